```python
import jax, jax.numpy as jnp
from jax import lax
import numpy as np

D_MODEL = 1024
BATCH = 8
SEQ = 4096
DEPTH = 1

D_MIX = D_MODEL
D_CONV = D_MIX // 2
CONV_GROUPS = 8
CONV_WIDTH = 31
D_HGRN = D_MIX - D_CONV
HGRN_HEADS = 4
HGRN_HEAD_DIM = D_HGRN // HGRN_HEADS
HGRN_CHUNK = 64
D_IN = 2 * D_CONV + 4 * D_HGRN
PEER_HEADS = 8
PEER_KEY_DIM = 256
PEER_N_KEYS = 128
PEER_N_EXPERTS = PEER_N_KEYS * PEER_N_KEYS
PEER_TOPK = 16
PEER_BLOCK = 128
EPS = 1e-6

kernel_name = "hybrid_conv_hgrn2_peer_block"


def rmsnorm(x, g):
    xf = x.astype(jnp.float32)
    xf = xf * lax.rsqrt(jnp.mean(xf * xf, axis=-1, keepdims=True) + EPS)
    return (xf * g.astype(jnp.float32)).astype(x.dtype)


def conformer_conv(a, gate, conv_w, conv_b, norm_g, norm_b):
    bsz, s, _ = a.shape
    h = a * jax.nn.sigmoid(gate)
    h = lax.conv_general_dilated(
        h, conv_w[:, None, :].astype(h.dtype), window_strides=(1,),
        padding=[(CONV_WIDTH - 1, 0)],
        dimension_numbers=('NWC', 'WIO', 'NWC'),
        feature_group_count=D_CONV) + conv_b.astype(h.dtype)
    hf = h.astype(jnp.float32).reshape(bsz, s, CONV_GROUPS, D_CONV // CONV_GROUPS)
    mu = jnp.mean(hf, axis=-1, keepdims=True)
    var = jnp.mean(jnp.square(hf - mu), axis=-1, keepdims=True)
    hf = ((hf - mu) * lax.rsqrt(var + EPS)).reshape(bsz, s, D_CONV)
    hf = hf * norm_g.astype(jnp.float32) + norm_b.astype(jnp.float32)
    return jax.nn.silu(hf).astype(a.dtype)


def _hgrn2_chunk(state, inp):
    q, k, v, log_f = inp
    b = jnp.cumsum(log_f, axis=2)
    inter = jnp.einsum('bhtk,bhkv->bhtv', q * jnp.exp(b), state)
    tri = jnp.tril(jnp.ones((HGRN_CHUNK, HGRN_CHUNK), dtype=bool))
    diff = b[:, :, :, None, :] - b[:, :, None, :, :]
    decay = jnp.exp(jnp.where(tri[:, :, None], diff, -jnp.inf))
    scores = jnp.einsum('bhtk,bhsk,bhtsk->bhts', q, k, decay)
    intra = jnp.einsum('bhts,bhsv->bhtv', scores, v)
    b_last = b[:, :, -1, :]
    state = (jnp.exp(b_last)[..., None] * state
             + jnp.einsum('bhsk,bhsv->bhkv', k * jnp.exp(b_last[:, :, None, :] - b), v))
    return state, inter + intra


def hgrn2(q_in, f_in, i_in, g_in, lb, norm_g):
    bsz, s, _ = q_in.shape
    shp = (bsz, s, HGRN_HEADS, HGRN_HEAD_DIM)
    lbh = lb.reshape(HGRN_HEADS, HGRN_HEAD_DIM)
    z = f_in.astype(jnp.float32).reshape(shp)
    f = lbh + (1.0 - lbh) * jax.nn.sigmoid(z)
    log_f = jnp.log(f)
    k = (1.0 - lbh) * jax.nn.sigmoid(-z)
    q = jax.nn.silu(q_in.astype(jnp.float32)).reshape(shp)
    v = i_in.astype(jnp.float32).reshape(shp)
    nc = s // HGRN_CHUNK

    def to_chunks(t):
        return t.reshape(bsz, nc, HGRN_CHUNK, HGRN_HEADS, HGRN_HEAD_DIM).transpose(1, 0, 3, 2, 4)

    state0 = jnp.zeros((bsz, HGRN_HEADS, HGRN_HEAD_DIM, HGRN_HEAD_DIM), jnp.float32)
    _, o = lax.scan(_hgrn2_chunk, state0,
                    (to_chunks(q), to_chunks(k), to_chunks(v), to_chunks(log_f)))
    o = o.transpose(1, 0, 3, 2, 4).reshape(shp)
    o = o * lax.rsqrt(jnp.mean(o * o, axis=-1, keepdims=True) + EPS)
    o = o.reshape(bsz, s, D_HGRN) * norm_g.astype(jnp.float32)
    o = o * jax.nn.silu(g_in.astype(jnp.float32))
    return o.astype(q_in.dtype)


def peer(xn, w_query, sub_keys, u_tab, v_tab):
    bsz, s, d = xn.shape
    xt = xn.reshape(-1, d)
    t = xt.shape[0]
    q = (xt @ w_query).reshape(t, PEER_HEADS, 2, PEER_KEY_DIM // 2)
    sc = jnp.einsum('thpd,hpnd->thpn', q, sub_keys).astype(jnp.float32)
    s1, i1 = lax.top_k(sc[:, :, 0], PEER_TOPK)
    s2, i2 = lax.top_k(sc[:, :, 1], PEER_TOPK)
    cand = (s1[..., :, None] + s2[..., None, :]).reshape(t, PEER_HEADS, PEER_TOPK * PEER_TOPK)
    top_s, ci = lax.top_k(cand, PEER_TOPK)
    experts = (jnp.take_along_axis(i1, ci // PEER_TOPK, axis=-1) * PEER_N_KEYS
               + jnp.take_along_axis(i2, ci % PEER_TOPK, axis=-1))
    gate = jax.nn.softmax(top_s, axis=-1).astype(xn.dtype)
    nb = t // PEER_BLOCK

    def block(args):
        xb, eb, gb = args
        u = u_tab[eb]
        act = jax.nn.gelu(jnp.einsum('thkd,td->thk', u, xb), approximate=False)
        return jnp.einsum('thk,thkd->td', gb * act, v_tab[eb])

    y = lax.map(block, (xt.reshape(nb, PEER_BLOCK, d),
                        experts.reshape(nb, PEER_BLOCK, PEER_HEADS, PEER_TOPK),
                        gate.reshape(nb, PEER_BLOCK, PEER_HEADS, PEER_TOPK)))
    return y.reshape(bsz, s, d)


def setup_inputs(seed: int = 0) -> dict:
    key = jax.random.key(seed)
    ks = jax.random.split(key, 16)
    f32 = jnp.float32
    nrm = lambda k, shp, sc: jax.random.normal(k, shp, f32) * sc
    return {
        "x": jax.random.normal(ks[0], (BATCH, SEQ, D_MODEL), f32),
        "norm_mix_g": 1.0 + nrm(ks[1], (DEPTH, D_MODEL), 0.02),
        "w_in": nrm(ks[2], (DEPTH, D_MODEL, D_IN), D_MODEL ** -0.5),
        "conv_w": nrm(ks[3], (DEPTH, CONV_WIDTH, D_CONV), CONV_WIDTH ** -0.5),
        "conv_b": nrm(ks[4], (DEPTH, D_CONV), 0.02),
        "conv_norm_g": 1.0 + nrm(ks[5], (DEPTH, D_CONV), 0.02),
        "conv_norm_b": nrm(ks[6], (DEPTH, D_CONV), 0.02),
        "hgrn_lb_logits": nrm(ks[7], (DEPTH + 1, D_HGRN), 0.1),
        "hgrn_norm_g": 1.0 + nrm(ks[8], (DEPTH, D_HGRN), 0.02),
        "w_out": nrm(ks[9], (DEPTH, D_MIX, D_MODEL), D_MIX ** -0.5),
        "norm_ffn_g": 1.0 + nrm(ks[10], (DEPTH, D_MODEL), 0.02),
        "peer_w_query": nrm(ks[11], (DEPTH, D_MODEL, PEER_HEADS * PEER_KEY_DIM), D_MODEL ** -0.5),
        "peer_sub_keys": nrm(ks[12], (DEPTH, PEER_HEADS, 2, PEER_N_KEYS, PEER_KEY_DIM // 2),
                             (PEER_KEY_DIM // 2) ** -0.5),
        "peer_u": nrm(ks[13], (DEPTH, PEER_N_EXPERTS, D_MODEL), D_MODEL ** -0.5),
        "peer_v": nrm(ks[14], (DEPTH, PEER_N_EXPERTS, D_MODEL), 0.5),
        "final_norm_g": 1.0 + nrm(ks[15], (D_MODEL,), 0.02),
    }


def reference(x, norm_mix_g, w_in, conv_w, conv_b, conv_norm_g, conv_norm_b, hgrn_lb_logits,
              hgrn_norm_g, w_out, norm_ffn_g, peer_w_query, peer_sub_keys, peer_u, peer_v,
              final_norm_g):
    lb_all = jnp.cumsum(jax.nn.softmax(hgrn_lb_logits.astype(jnp.float32), axis=0), axis=0)
    splits = [D_CONV, 2 * D_CONV, 2 * D_CONV + D_HGRN, 2 * D_CONV + 2 * D_HGRN,
              2 * D_CONV + 3 * D_HGRN]
    h = x
    for l in range(DEPTH):
        xn = rmsnorm(h, norm_mix_g[l])
        proj = xn @ w_in[l]
        a, a_gate, qh, fh, ih, gh = jnp.split(proj, splits, axis=-1)
        y_conv = conformer_conv(a, a_gate, conv_w[l], conv_b[l], conv_norm_g[l], conv_norm_b[l])
        y_hgrn = hgrn2(qh, fh, ih, gh, lb_all[l], hgrn_norm_g[l])
        h = h + jnp.concatenate([y_conv, y_hgrn], axis=-1) @ w_out[l]
        h = h + peer(rmsnorm(h, norm_ffn_g[l]), peer_w_query[l], peer_sub_keys[l],
                     peer_u[l], peer_v[l])
    return rmsnorm(h, final_norm_g)
```

```python
import functools

import numpy as np
import jax
import jax.numpy as jnp
from jax import lax
from jax.experimental import pallas as pl
from jax.experimental.pallas import tpu as pltpu

F32 = jnp.float32
BF16 = jnp.bfloat16
EPS = 1e-6

LANES = 128
SUBLANES = 8
CONV_GROUPS = 8
HGRN_CHUNK = 64
PEER_TOPK = 16
VMEM_LIMIT = 48 * 1024 * 1024

_NT = (((1,), (1,)), ((), ()))
_TN = (((0,), (0,)), ((), ()))


def _cparams(sem):
    return pltpu.CompilerParams(dimension_semantics=sem, vmem_limit_bytes=VMEM_LIMIT)


def _split2(x):
    hi = x.astype(BF16)
    lo = (x - hi.astype(F32)).astype(BF16)
    return hi, lo


def _dot2(x, w):
    hi, lo = _split2(x)
    return (jnp.dot(hi, w, preferred_element_type=F32)
            + jnp.dot(lo, w, preferred_element_type=F32))


def _inproj_kernel(x_ref, g_ref, w_ref, o_ref):
    x = x_ref[...]
    xn = x * lax.rsqrt(jnp.mean(x * x, axis=-1, keepdims=True) + EPS) * g_ref[...]
    o_ref[...] = jnp.dot(xn.astype(BF16), w_ref[...], preferred_element_type=F32)


def _inproj(x2, g, w_bf, tm=256):
    t, d = x2.shape
    n = w_bf.shape[1]
    return pl.pallas_call(
        _inproj_kernel,
        grid=(t // tm,),
        in_specs=[pl.BlockSpec((tm, d), lambda i: (i, 0)),
                  pl.BlockSpec((1, d), lambda i: (0, 0)),
                  pl.BlockSpec((d, n), lambda i: (0, 0))],
        out_specs=pl.BlockSpec((tm, n), lambda i: (i, 0)),
        out_shape=jax.ShapeDtypeStruct((t, n), F32),
        compiler_params=_cparams(("arbitrary",)),
        name="inproj",
    )(x2, g, w_bf)


_CONV_SUB = 32
_CONV_HIST = 32


def _conv_kernel(p_ref, w_ref, cb_ref, ng_ref, nb_ref, gm_ref, o_ref, hbuf, pre, *, width):
    tm, dc = pre.shape
    j = pl.program_id(1)

    @pl.when(j == 0)
    def _():
        hbuf[0:_CONV_HIST, :] = jnp.zeros((_CONV_HIST, dc), F32)

    @pl.when(j > 0)
    def _():
        hbuf[0:_CONV_HIST, :] = hbuf[tm:tm + _CONV_HIST, :]

    a = p_ref[:, 0:dc]
    gate = p_ref[:, dc:2 * dc]
    hbuf[_CONV_HIST:_CONV_HIST + tm, :] = a * jax.nn.sigmoid(gate)

    first = _CONV_HIST - (width - 1)

    def body(r, carry):
        base = pl.multiple_of(r * _CONV_SUB, _CONV_SUB)
        win = hbuf[pl.ds(base, 2 * _CONV_SUB), :]
        acc = jnp.broadcast_to(cb_ref[...], (_CONV_SUB, dc))
        for ph in range(SUBLANES):
            offs = [first + k for k in range(width) if (first + k) % SUBLANES == ph]
            if not offs:
                continue
            span = max(offs) - ph + _CONV_SUB
            sh = win[ph:ph + span, :]
            for o in offs:
                k = o - first
                acc = acc + sh[o - ph:o - ph + _CONV_SUB, :] * w_ref[k:k + 1, :]
        pre[pl.ds(base, _CONV_SUB), :] = acc
        return carry

    lax.fori_loop(0, tm // _CONV_SUB, body, 0)

    h = pre[...]
    gm = gm_ref[...]
    mu = _dot2(h, gm)
    d = h - mu
    var = _dot2(d * d, gm)
    hn = d * lax.rsqrt(var + EPS) * ng_ref[...] + nb_ref[...]
    o_ref[...] = (hn * jax.nn.sigmoid(hn)).astype(o_ref.dtype)


def _conv(proj3, conv_w, conv_b, norm_g, norm_b, tm=512):
    b, s, _ = proj3.shape
    width, dc = conv_w.shape
    assert width - 1 <= _CONV_HIST and tm % _CONV_SUB == 0 and s % tm == 0
    gsz = dc // CONV_GROUPS
    gid = np.arange(dc) // gsz
    gm = jnp.asarray((gid[:, None] == gid[None, :]).astype(np.float32) / gsz, BF16)
    return pl.pallas_call(
        functools.partial(_conv_kernel, width=width),
        grid=(b, s // tm),
        in_specs=[pl.BlockSpec((None, tm, 2 * dc), lambda i, j: (i, j, 0)),
                  pl.BlockSpec((width, dc), lambda i, j: (0, 0)),
                  pl.BlockSpec((1, dc), lambda i, j: (0, 0)),
                  pl.BlockSpec((1, dc), lambda i, j: (0, 0)),
                  pl.BlockSpec((1, dc), lambda i, j: (0, 0)),
                  pl.BlockSpec((dc, dc), lambda i, j: (0, 0))],
        out_specs=pl.BlockSpec((None, tm, dc), lambda i, j: (i, j, 0)),
        out_shape=jax.ShapeDtypeStruct((b, s, dc), BF16),
        scratch_shapes=[pltpu.VMEM((tm + _CONV_HIST, dc), F32),
                        pltpu.VMEM((tm, dc), F32)],
        compiler_params=_cparams(("arbitrary", "arbitrary")),
        name="conformer_conv",
    )(proj3, conv_w, conv_b, norm_g, norm_b, gm)


def _hgrn_tables(c):
    r = np.arange(c)
    mats = [(r[None, :] <= r[:, None]), (r[None, :] > r[:, None])]
    masks = []
    m = 1
    while m < c:
        blk, pos = r // (2 * m), r % (2 * m)
        anchor = blk * 2 * m + m - 1
        second = pos >= m
        p = np.where(second[:, None],
                     (r[None, :] > anchor[:, None]) & (r[None, :] <= r[:, None]),
                     (r[None, :] > r[:, None]) & (r[None, :] <= anchor[:, None]))
        mats.append(p)
        masks.append((blk[:, None] == blk[None, :]) & second[:, None] & (~second)[None, :])
        m *= 2
    pm = np.concatenate(mats, axis=0).astype(np.float32)
    return jnp.asarray(pm, BF16), jnp.asarray(np.stack(masks).astype(np.float32))


def _hgrn_kernel(qf_ref, ig_ref, lb_ref, ng_ref, pm_ref, mask_ref, o_ref, st_ref, *, heads):
    c, dh = o_ref.shape
    hd = dh // heads
    nlev = mask_ref.shape[0]

    @pl.when(pl.program_id(1) == 0)
    def _():
        st_ref[...] = jnp.zeros(st_ref.shape, F32)

    lb = lb_ref[...]
    qin = qf_ref[:, 0:dh]
    z = qf_ref[:, dh:2 * dh]
    vin = ig_ref[:, 0:dh]
    gin = ig_ref[:, dh:2 * dh]

    f = lb + (1.0 - lb) * jax.nn.sigmoid(z)
    logf = jnp.log(f)
    kk = (1.0 - lb) * jax.nn.sigmoid(-z)
    q = qin * jax.nn.sigmoid(qin)

    pm = pm_ref[...]
    hi = logf.astype(BF16)
    r1 = logf - hi.astype(F32)
    mid = r1.astype(BF16)
    lo = (r1 - mid.astype(F32)).astype(BF16)
    ex = (jnp.dot(pm, hi, preferred_element_type=F32)
          + jnp.dot(pm, mid, preferred_element_type=F32)
          + jnp.dot(pm, lo, preferred_element_type=F32))
    b = ex[0:c]
    suf = ex[c:2 * c]

    for h in range(heads):
        sl = slice(h * hd, (h + 1) * hd)
        qh, kh, vh = q[:, sl], kk[:, sl], vin[:, sl]
        vb = vh.astype(BF16)
        st = st_ref[h]
        qb = (qh * jnp.exp(b[:, sl])).astype(BF16)
        o = lax.dot_general(qb, st.astype(BF16), _NT, preferred_element_type=F32)
        scores = jnp.zeros((c, c), F32)
        for lv in range(nlev):
            el = jnp.exp(ex[(2 + lv) * c:(3 + lv) * c, sl])
            s = lax.dot_general((qh * el).astype(BF16), (kh * el).astype(BF16), _NT,
                                preferred_element_type=F32)
            scores = scores + mask_ref[lv] * s
        o = o + jnp.dot(scores.astype(BF16), vb, preferred_element_type=F32)
        o = o + jnp.sum(qh * kh, axis=-1, keepdims=True) * vh
        kd = (kh * jnp.exp(suf[:, sl])).astype(BF16)
        st_ref[h] = (st * jnp.exp(b[c - 1:c, sl])
                     + lax.dot_general(vb, kd, _TN, preferred_element_type=F32))
        o = o * lax.rsqrt(jnp.mean(o * o, axis=-1, keepdims=True) + EPS) * ng_ref[:, sl]
        gh = gin[:, sl]
        o_ref[:, sl] = (o * (gh * jax.nn.sigmoid(gh))).astype(o_ref.dtype)


def _hgrn(proj3, lb, norm_g, heads, c=HGRN_CHUNK):
    b, s, n = proj3.shape
    dh = lb.shape[1]
    nblk = n // (2 * dh)
    pm, masks = _hgrn_tables(c)
    return pl.pallas_call(
        functools.partial(_hgrn_kernel, heads=heads),
        grid=(b, s // c),
        in_specs=[pl.BlockSpec((None, c, 2 * dh), lambda i, j: (i, j, nblk - 2)),
                  pl.BlockSpec((None, c, 2 * dh), lambda i, j: (i, j, nblk - 1)),
                  pl.BlockSpec((1, dh), lambda i, j: (0, 0)),
                  pl.BlockSpec((1, dh), lambda i, j: (0, 0)),
                  pl.BlockSpec(pm.shape, lambda i, j: (0, 0)),
                  pl.BlockSpec(masks.shape, lambda i, j: (0, 0, 0))],
        out_specs=pl.BlockSpec((None, c, dh), lambda i, j: (i, j, 0)),
        out_shape=jax.ShapeDtypeStruct((b, s, dh), BF16),
        scratch_shapes=[pltpu.VMEM((heads, dh // heads, dh // heads), F32)],
        compiler_params=_cparams(("arbitrary", "arbitrary")),
        name="hgrn2",
    )(proj3, proj3, lb, norm_g, pm, masks)


def _extract_topk(sc, k, payload=None):
    n = sc.shape[0]
    iota = lax.broadcasted_iota(jnp.int32, sc.shape, 0)
    vals, picks = [], []
    for _ in range(k):
        mx = jnp.max(sc, axis=0, keepdims=True)
        pos = jnp.min(jnp.where(sc == mx, iota, n), axis=0, keepdims=True)
        hit = iota == pos
        vals.append(mx)
        if payload is None:
            picks.append(pos)
        else:
            picks.append(jnp.max(jnp.where(hit, payload, -1), axis=0, keepdims=True))
        sc = jnp.where(hit, -jnp.inf, sc)
    return jnp.concatenate(vals, axis=0), jnp.concatenate(picks, axis=0)


def _mid_kernel(x_ref, ya_ref, yb_ref, woa_ref, wob_ref, g_ref, wq_ref, keys_ref,
                h2_ref, xn_ref, eidx_ref, gate_ref, q_scr, *, n_keys):
    k = PEER_TOPK
    kq = int(np.sqrt(k))
    heads = eidx_ref.shape[0]
    kd = keys_ref.shape[2]

    h2 = (x_ref[...]
          + jnp.dot(ya_ref[...], woa_ref[...], preferred_element_type=F32)
          + jnp.dot(yb_ref[...], wob_ref[...], preferred_element_type=F32))
    h2_ref[...] = h2
    xn = h2 * lax.rsqrt(jnp.mean(h2 * h2, axis=-1, keepdims=True) + EPS) * g_ref[...]
    xn_ref[...] = xn
    q_scr[...] = jnp.dot(xn.astype(BF16), wq_ref[...], preferred_element_type=F32)

    def head(h, carry):
        tops = []
        for p in range(2):
            col = pl.multiple_of((2 * h + p) * kd, kd)
            qhp = q_scr[:, pl.ds(col, kd)].astype(BF16)
            sc = lax.dot_general(keys_ref[2 * h + p], qhp, _NT,
                                 preferred_element_type=F32)
            tops.append(_extract_topk(sc, k))
        (v1, i1), (v2, i2) = tops
        low = lax.broadcasted_iota(jnp.int32, v1.shape, 0) < kq
        cand, ids = [], []
        for a in range(kq):
            cand.append(v1[a:a + 1] + v2)
            ids.append(i1[a:a + 1] * n_keys + i2)
        for bb in range(kq):
            cand.append(jnp.where(low, -jnp.inf, v1 + v2[bb:bb + 1]))
            ids.append(i1 * n_keys + i2[bb:bb + 1])
        tv, te = _extract_topk(jnp.concatenate(cand, axis=0), k,
                               payload=jnp.concatenate(ids, axis=0))
        ez = jnp.exp(tv - tv[0:1])
        eidx_ref[h] = te
        gate_ref[h] = ez / jnp.sum(ez, axis=0, keepdims=True)
        return carry

    lax.fori_loop(0, heads, head, 0)


def _mid(x2, ya, yb, wo_bf, g, wq_bf, keys_bf, tm=256):
    t, d = x2.shape
    dc = ya.shape[1]
    hp, n_keys, kd = keys_bf.shape
    heads = hp // 2
    nq = wq_bf.shape[1]
    k = PEER_TOPK
    return pl.pallas_call(
        functools.partial(_mid_kernel, n_keys=n_keys),
        grid=(t // tm,),
        in_specs=[pl.BlockSpec((tm, d), lambda i: (i, 0)),
                  pl.BlockSpec((tm, dc), lambda i: (i, 0)),
                  pl.BlockSpec((tm, d - dc), lambda i: (i, 0)),
                  pl.BlockSpec((dc, d), lambda i: (0, 0)),
                  pl.BlockSpec((d - dc, d), lambda i: (0, 0)),
                  pl.BlockSpec((1, d), lambda i: (0, 0)),
                  pl.BlockSpec((d, nq), lambda i: (0, 0)),
                  pl.BlockSpec((hp, n_keys, kd), lambda i: (0, 0, 0))],
        out_specs=[pl.BlockSpec((tm, d), lambda i: (i, 0)),
                   pl.BlockSpec((tm, d), lambda i: (i, 0)),
                   pl.BlockSpec((heads, k, tm), lambda i: (0, 0, i)),
                   pl.BlockSpec((heads, k, tm), lambda i: (0, 0, i))],
        out_shape=[jax.ShapeDtypeStruct((t, d), F32),
                   jax.ShapeDtypeStruct((t, d), F32),
                   jax.ShapeDtypeStruct((heads, k, t), jnp.int32),
                   jax.ShapeDtypeStruct((heads, k, t), F32)],
        scratch_shapes=[pltpu.VMEM((tm, nq), F32)],
        compiler_params=_cparams(("arbitrary",)),
        name="mid",
    )(x2, ya, yb, wo_bf[:dc], wo_bf[dc:], g, wq_bf, keys_bf)


def _peer_consts(picks):
    col = np.arange(picks * SUBLANES)
    diag = (col[None, :] % SUBLANES == np.arange(SUBLANES)[:, None]).astype(np.float32)
    grp = (col[:, None] // SUBLANES == np.arange(picks)[None, :]).astype(np.float32)
    return jnp.asarray(diag), jnp.asarray(grp, BF16), jnp.asarray(grp.T, BF16)


def _gather_rows(eidx_ref, tab_ref, gb_ref, t, picks):
    for j in range(picks):
        row = pl.multiple_of(eidx_ref[t, j] * SUBLANES, SUBLANES)
        gb_ref[j * SUBLANES:(j + 1) * SUBLANES, :] = tab_ref[pl.ds(row, SUBLANES), :]


def _peer_down_kernel(eidx_ref, x_ref, gate_ref, tab_ref, diag_ref, grp_ref, w_ref, gb_ref, z_ref):
    tb, picks = gate_ref.shape

    def token(t, carry):
        _gather_rows(eidx_ref, tab_ref, gb_ref, t, picks)
        xhi, xlo = _split2(x_ref[t])
        w = gb_ref[...]
        part = (lax.dot_general(xhi, w, _NT, preferred_element_type=F32)
                + lax.dot_general(xlo, w, _NT, preferred_element_type=F32))
        z_ref[t] = part * diag_ref[...]
        return carry

    lax.fori_loop(0, tb, token, 0)
    z = z_ref[...].reshape(tb * SUBLANES, picks * SUBLANES)
    act = jnp.sum(_dot2(z, grp_ref[...]).reshape(tb, SUBLANES, picks), axis=1)
    gelu = 0.5 * act * (1.0 + lax.erf(act * np.float32(np.sqrt(0.5))))
    w_ref[...] = gate_ref[...] * gelu


def _peer_up_kernel(eidx_ref, w_in_ref, h2_ref, g_ref, tab_ref, diag_ref, grpt_ref, o_ref,
                    gb_ref, wx_ref, *, d):
    tb, picks = w_in_ref.shape
    wx_ref[...] = _dot2(w_in_ref[...], grpt_ref[...])

    def token(t, carry):
        _gather_rows(eidx_ref, tab_ref, gb_ref, t, picks)
        coef = wx_ref[pl.ds(t, 1), :] * diag_ref[...]
        chi, clo = _split2(coef)
        v = gb_ref[...]
        y = (jnp.dot(chi, v, preferred_element_type=F32)
             + jnp.dot(clo, v, preferred_element_type=F32))
        o_ref[t] = h2_ref[t] + y
        return carry

    lax.fori_loop(0, tb, token, 0)
    h = o_ref[...]
    ms = jnp.sum(jnp.sum(h * h, axis=2, keepdims=True), axis=1, keepdims=True) / d
    o_ref[...] = h * lax.rsqrt(ms + EPS) * g_ref[...]


def _table_spec(tab):
    return pl.BlockSpec(tab.shape, lambda i: (0, 0), pipeline_mode=pl.Buffered(1))


def _peer(xn2, h2, eidx_t, gate_t, u_tab, v_tab, final_g, tb=32):
    t, d = xn2.shape
    picks = eidx_t.shape[1]
    assert d == SUBLANES * LANES
    diag, grp, grpt = _peer_consts(picks)
    x8 = xn2.reshape(t, SUBLANES, LANES)
    h8 = h2.reshape(t, SUBLANES, LANES)
    g8 = final_g.reshape(1, SUBLANES, LANES).astype(F32)
    smem_idx = pl.BlockSpec((tb, picks), lambda i: (i, 0), memory_space=pltpu.SMEM)
    row3 = pl.BlockSpec((tb, SUBLANES, LANES), lambda i: (i, 0, 0))
    row2 = pl.BlockSpec((tb, picks), lambda i: (i, 0))
    const2 = lambda a: pl.BlockSpec(a.shape, lambda i: (0, 0))

    w = pl.pallas_call(
        _peer_down_kernel,
        grid=(t // tb,),
        in_specs=[smem_idx, row3, row2, _table_spec(u_tab), const2(diag), const2(grp)],
        out_specs=row2,
        out_shape=jax.ShapeDtypeStruct((t, picks), F32),
        scratch_shapes=[pltpu.VMEM((picks * SUBLANES, LANES), BF16),
                        pltpu.VMEM((tb, SUBLANES, picks * SUBLANES), F32)],
        compiler_params=_cparams(("arbitrary",)),
        name="peer_down",
    )(eidx_t, x8, gate_t, u_tab, diag, grp)

    out = pl.pallas_call(
        functools.partial(_peer_up_kernel, d=d),
        grid=(t // tb,),
        in_specs=[smem_idx, row2, row3, pl.BlockSpec((1, SUBLANES, LANES), lambda i: (0, 0, 0)),
                  _table_spec(v_tab), const2(diag), const2(grpt)],
        out_specs=row3,
        out_shape=jax.ShapeDtypeStruct((t, SUBLANES, LANES), F32),
        scratch_shapes=[pltpu.VMEM((picks * SUBLANES, LANES), BF16),
                        pltpu.VMEM((tb, picks * SUBLANES), F32)],
        compiler_params=_cparams(("arbitrary",)),
        name="peer_up",
    )(eidx_t, w, h8, g8, v_tab, diag, grpt)
    return out.reshape(t, d)


def kernel(x, norm_mix_g, w_in, conv_w, conv_b, conv_norm_g, conv_norm_b, hgrn_lb_logits,
           hgrn_norm_g, w_out, norm_ffn_g, peer_w_query, peer_sub_keys, peer_u, peer_v,
           final_norm_g):
    bsz, s, d = x.shape
    depth = w_in.shape[0]
    dc = conv_w.shape[2]
    dh = hgrn_norm_g.shape[1]
    heads_h = dh // LANES
    ph, _, n_keys, kd = peer_sub_keys.shape[1:]
    t = bsz * s
    row = lambda a: a.reshape(1, -1).astype(F32)

    lb_all = jnp.cumsum(jax.nn.softmax(hgrn_lb_logits.astype(F32), axis=0), axis=0)
    h = x.reshape(t, d)
    for l in range(depth):
        proj = _inproj(h, row(norm_mix_g[l]), w_in[l].astype(BF16))
        proj3 = proj.reshape(bsz, s, -1)
        ya = _conv(proj3, conv_w[l], row(conv_b[l]), row(conv_norm_g[l]), row(conv_norm_b[l]))
        yb = _hgrn(proj3, row(lb_all[l]), row(hgrn_norm_g[l]), heads_h)
        h2, xn2, eidx, gate = _mid(h, ya.reshape(t, dc), yb.reshape(t, dh), w_out[l].astype(BF16),
                                   row(norm_ffn_g[l]), peer_w_query[l].astype(BF16),
                                   peer_sub_keys[l].reshape(ph * 2, n_keys, kd).astype(BF16))
        picks = ph * PEER_TOPK
        eidx_t = eidx.reshape(picks, t).T
        gate_t = gate.reshape(picks, t).T
        u_tab = peer_u[l].astype(BF16).reshape(-1, LANES)
        v_tab = peer_v[l].astype(BF16).reshape(-1, LANES)
        last = l == depth - 1
        g_out = final_norm_g if last else jnp.ones((d,), F32)
        h = _peer(xn2, h2, eidx_t, gate_t, u_tab, v_tab, g_out)
        assert last, "a deeper stack needs the un-normalised residual between layers"
    return h.reshape(bsz, s, d)
```

```python
import functools

import numpy as np
import jax
import jax.numpy as jnp
from jax import lax
from jax.experimental import pallas as pl
from jax.experimental.pallas import tpu as pltpu

F32 = jnp.float32
BF16 = jnp.bfloat16
EPS = 1e-6

LANES = 128
SUBLANES = 8
CONV_GROUPS = 8
HGRN_CHUNK = 64
PEER_TOPK = 16
VMEM_LIMIT = 48 * 1024 * 1024

_NT = (((1,), (1,)), ((), ()))
_TN = (((0,), (0,)), ((), ()))


def _cparams(sem):
    return pltpu.CompilerParams(dimension_semantics=sem, vmem_limit_bytes=VMEM_LIMIT)


def _split2(x):
    hi = x.astype(BF16)
    lo = (x - hi.astype(F32)).astype(BF16)
    return hi, lo


def _dot2(x, w):
    hi, lo = _split2(x)
    return (jnp.dot(hi, w, preferred_element_type=F32)
            + jnp.dot(lo, w, preferred_element_type=F32))


def _inproj_kernel(x_ref, g_ref, w_ref, o_ref):
    x = x_ref[...]
    xn = x * lax.rsqrt(jnp.mean(x * x, axis=-1, keepdims=True) + EPS) * g_ref[...]
    o_ref[...] = jnp.dot(xn.astype(BF16), w_ref[...], preferred_element_type=F32)


def _inproj(x2, g, w_bf, tm=256):
    t, d = x2.shape
    n = w_bf.shape[1]
    return pl.pallas_call(
        _inproj_kernel,
        grid=(t // tm,),
        in_specs=[pl.BlockSpec((tm, d), lambda i: (i, 0)),
                  pl.BlockSpec((1, d), lambda i: (0, 0)),
                  pl.BlockSpec((d, n), lambda i: (0, 0))],
        out_specs=pl.BlockSpec((tm, n), lambda i: (i, 0)),
        out_shape=jax.ShapeDtypeStruct((t, n), F32),
        compiler_params=_cparams(("arbitrary",)),
        name="inproj",
    )(x2, g, w_bf)


_CONV_SUB = 32
_CONV_HIST = 32


def _conv_kernel(p_ref, w_ref, cb_ref, ng_ref, nb_ref, gm_ref, o_ref, hbuf, pre, *, width):
    tm, dc = pre.shape
    j = pl.program_id(1)

    @pl.when(j == 0)
    def _():
        hbuf[0:_CONV_HIST, :] = jnp.zeros((_CONV_HIST, dc), F32)

    @pl.when(j > 0)
    def _():
        hbuf[0:_CONV_HIST, :] = hbuf[tm:tm + _CONV_HIST, :]

    a = p_ref[:, 0:dc]
    gate = p_ref[:, dc:2 * dc]
    hbuf[_CONV_HIST:_CONV_HIST + tm, :] = a * jax.nn.sigmoid(gate)

    first = _CONV_HIST - (width - 1)

    def body(r, carry):
        base = pl.multiple_of(r * _CONV_SUB, _CONV_SUB)
        win = hbuf[pl.ds(base, 2 * _CONV_SUB), :]
        acc = jnp.broadcast_to(cb_ref[...], (_CONV_SUB, dc))
        for ph in range(SUBLANES):
            offs = [first + k for k in range(width) if (first + k) % SUBLANES == ph]
            if not offs:
                continue
            span = max(offs) - ph + _CONV_SUB
            sh = win[ph:ph + span, :]
            for o in offs:
                k = o - first
                acc = acc + sh[o - ph:o - ph + _CONV_SUB, :] * w_ref[k:k + 1, :]
        pre[pl.ds(base, _CONV_SUB), :] = acc
        return carry

    lax.fori_loop(0, tm // _CONV_SUB, body, 0)

    h = pre[...]
    gm = gm_ref[...]
    mu = _dot2(h, gm)
    d = h - mu
    var = _dot2(d * d, gm)
    hn = d * lax.rsqrt(var + EPS) * ng_ref[...] + nb_ref[...]
    o_ref[...] = (hn * jax.nn.sigmoid(hn)).astype(o_ref.dtype)


def _conv(proj3, conv_w, conv_b, norm_g, norm_b, tm=512):
    b, s, _ = proj3.shape
    width, dc = conv_w.shape
    assert width - 1 <= _CONV_HIST and tm % _CONV_SUB == 0 and s % tm == 0
    gsz = dc // CONV_GROUPS
    gid = np.arange(dc) // gsz
    gm = jnp.asarray((gid[:, None] == gid[None, :]).astype(np.float32) / gsz, BF16)
    return pl.pallas_call(
        functools.partial(_conv_kernel, width=width),
        grid=(b, s // tm),
        in_specs=[pl.BlockSpec((None, tm, 2 * dc), lambda i, j: (i, j, 0)),
                  pl.BlockSpec((width, dc), lambda i, j: (0, 0)),
                  pl.BlockSpec((1, dc), lambda i, j: (0, 0)),
                  pl.BlockSpec((1, dc), lambda i, j: (0, 0)),
                  pl.BlockSpec((1, dc), lambda i, j: (0, 0)),
                  pl.BlockSpec((dc, dc), lambda i, j: (0, 0))],
        out_specs=pl.BlockSpec((None, tm, dc), lambda i, j: (i, j, 0)),
        out_shape=jax.ShapeDtypeStruct((b, s, dc), BF16),
        scratch_shapes=[pltpu.VMEM((tm + _CONV_HIST, dc), F32),
                        pltpu.VMEM((tm, dc), F32)],
        compiler_params=_cparams(("arbitrary", "arbitrary")),
        name="conformer_conv",
    )(proj3, conv_w, conv_b, norm_g, norm_b, gm)


def _hgrn_tables(c):
    r = np.arange(c)
    mats = [(r[None, :] <= r[:, None]), (r[None, :] > r[:, None])]
    masks = []
    m = 1
    while m < c:
        blk, pos = r // (2 * m), r % (2 * m)
        anchor = blk * 2 * m + m - 1
        second = pos >= m
        p = np.where(second[:, None],
                     (r[None, :] > anchor[:, None]) & (r[None, :] <= r[:, None]),
                     (r[None, :] > r[:, None]) & (r[None, :] <= anchor[:, None]))
        mats.append(p)
        masks.append((blk[:, None] == blk[None, :]) & second[:, None] & (~second)[None, :])
        m *= 2
    pm = np.concatenate(mats, axis=0).astype(np.float32)
    return jnp.asarray(pm, BF16), jnp.asarray(np.stack(masks).astype(np.float32))


def _hgrn_kernel(qf_ref, ig_ref, lb_ref, ng_ref, pm_ref, mask_ref, o_ref, st_ref, *, heads):
    c, dh = o_ref.shape
    hd = dh // heads
    nlev = mask_ref.shape[0]

    @pl.when(pl.program_id(1) == 0)
    def _():
        st_ref[...] = jnp.zeros(st_ref.shape, F32)

    lb = lb_ref[...]
    qin = qf_ref[:, 0:dh]
    z = qf_ref[:, dh:2 * dh]
    vin = ig_ref[:, 0:dh]
    gin = ig_ref[:, dh:2 * dh]

    f = lb + (1.0 - lb) * jax.nn.sigmoid(z)
    logf = jnp.log(f)
    kk = (1.0 - lb) * jax.nn.sigmoid(-z)
    q = qin * jax.nn.sigmoid(qin)

    pm = pm_ref[...]
    hi = logf.astype(BF16)
    r1 = logf - hi.astype(F32)
    mid = r1.astype(BF16)
    lo = (r1 - mid.astype(F32)).astype(BF16)
    ex = (jnp.dot(pm, hi, preferred_element_type=F32)
          + jnp.dot(pm, mid, preferred_element_type=F32)
          + jnp.dot(pm, lo, preferred_element_type=F32))
    b = ex[0:c]
    suf = ex[c:2 * c]

    for h in range(heads):
        sl = slice(h * hd, (h + 1) * hd)
        qh, kh, vh = q[:, sl], kk[:, sl], vin[:, sl]
        vb = vh.astype(BF16)
        st = st_ref[h]
        qb = (qh * jnp.exp(b[:, sl])).astype(BF16)
        o = lax.dot_general(qb, st.astype(BF16), _NT, preferred_element_type=F32)
        scores = jnp.zeros((c, c), F32)
        for lv in range(nlev):
            el = jnp.exp(ex[(2 + lv) * c:(3 + lv) * c, sl])
            s = lax.dot_general((qh * el).astype(BF16), (kh * el).astype(BF16), _NT,
                                preferred_element_type=F32)
            scores = scores + mask_ref[lv] * s
        o = o + jnp.dot(scores.astype(BF16), vb, preferred_element_type=F32)
        o = o + jnp.sum(qh * kh, axis=-1, keepdims=True) * vh
        kd = (kh * jnp.exp(suf[:, sl])).astype(BF16)
        st_ref[h] = (st * jnp.exp(b[c - 1:c, sl])
                     + lax.dot_general(vb, kd, _TN, preferred_element_type=F32))
        o = o * lax.rsqrt(jnp.mean(o * o, axis=-1, keepdims=True) + EPS) * ng_ref[:, sl]
        gh = gin[:, sl]
        o_ref[:, sl] = (o * (gh * jax.nn.sigmoid(gh))).astype(o_ref.dtype)


def _hgrn(proj3, lb, norm_g, heads, c=HGRN_CHUNK):
    b, s, n = proj3.shape
    dh = lb.shape[1]
    nblk = n // (2 * dh)
    pm, masks = _hgrn_tables(c)
    return pl.pallas_call(
        functools.partial(_hgrn_kernel, heads=heads),
        grid=(b, s // c),
        in_specs=[pl.BlockSpec((None, c, 2 * dh), lambda i, j: (i, j, nblk - 2)),
                  pl.BlockSpec((None, c, 2 * dh), lambda i, j: (i, j, nblk - 1)),
                  pl.BlockSpec((1, dh), lambda i, j: (0, 0)),
                  pl.BlockSpec((1, dh), lambda i, j: (0, 0)),
                  pl.BlockSpec(pm.shape, lambda i, j: (0, 0)),
                  pl.BlockSpec(masks.shape, lambda i, j: (0, 0, 0))],
        out_specs=pl.BlockSpec((None, c, dh), lambda i, j: (i, j, 0)),
        out_shape=jax.ShapeDtypeStruct((b, s, dh), BF16),
        scratch_shapes=[pltpu.VMEM((heads, dh // heads, dh // heads), F32)],
        compiler_params=_cparams(("arbitrary", "arbitrary")),
        name="hgrn2",
    )(proj3, proj3, lb, norm_g, pm, masks)


def _extract_topk(sc, k, payload=None):
    n = sc.shape[0]
    iota = lax.broadcasted_iota(jnp.int32, sc.shape, 0)
    vals, picks = [], []
    for _ in range(k):
        mx = jnp.max(sc, axis=0, keepdims=True)
        pos = jnp.min(jnp.where(sc == mx, iota, n), axis=0, keepdims=True)
        hit = iota == pos
        vals.append(mx)
        if payload is None:
            picks.append(pos)
        else:
            picks.append(jnp.max(jnp.where(hit, payload, -1), axis=0, keepdims=True))
        sc = jnp.where(hit, -jnp.inf, sc)
    return jnp.concatenate(vals, axis=0), jnp.concatenate(picks, axis=0)


def _mid_kernel(x_ref, ya_ref, yb_ref, woa_ref, wob_ref, g_ref, wq_ref, keys_ref,
                h2_ref, xn_ref, eidx_ref, gate_ref, q_scr, *, n_keys):
    k = PEER_TOPK
    kq = int(np.sqrt(k))
    heads = eidx_ref.shape[0]
    kd = keys_ref.shape[2]

    h2 = (x_ref[...]
          + jnp.dot(ya_ref[...], woa_ref[...], preferred_element_type=F32)
          + jnp.dot(yb_ref[...], wob_ref[...], preferred_element_type=F32))
    h2_ref[...] = h2
    xn = h2 * lax.rsqrt(jnp.mean(h2 * h2, axis=-1, keepdims=True) + EPS) * g_ref[...]
    xn_ref[...] = xn
    q_scr[...] = jnp.dot(xn.astype(BF16), wq_ref[...], preferred_element_type=F32)

    def head(h, carry):
        tops = []
        for p in range(2):
            col = pl.multiple_of((2 * h + p) * kd, kd)
            qhp = q_scr[:, pl.ds(col, kd)].astype(BF16)
            sc = lax.dot_general(keys_ref[2 * h + p], qhp, _NT,
                                 preferred_element_type=F32)
            tops.append(_extract_topk(sc, k))
        (v1, i1), (v2, i2) = tops
        low = lax.broadcasted_iota(jnp.int32, v1.shape, 0) < kq
        cand, ids = [], []
        for a in range(kq):
            cand.append(v1[a:a + 1] + v2)
            ids.append(i1[a:a + 1] * n_keys + i2)
        for bb in range(kq):
            cand.append(jnp.where(low, -jnp.inf, v1 + v2[bb:bb + 1]))
            ids.append(i1 * n_keys + i2[bb:bb + 1])
        tv, te = _extract_topk(jnp.concatenate(cand, axis=0), k,
                               payload=jnp.concatenate(ids, axis=0))
        ez = jnp.exp(tv - tv[0:1])
        eidx_ref[h] = te
        gate_ref[h] = ez / jnp.sum(ez, axis=0, keepdims=True)
        return carry

    lax.fori_loop(0, heads, head, 0)


def _mid(x2, ya, yb, wo_bf, g, wq_bf, keys_bf, tm=256):
    t, d = x2.shape
    dc = ya.shape[1]
    hp, n_keys, kd = keys_bf.shape
    heads = hp // 2
    nq = wq_bf.shape[1]
    k = PEER_TOPK
    return pl.pallas_call(
        functools.partial(_mid_kernel, n_keys=n_keys),
        grid=(t // tm,),
        in_specs=[pl.BlockSpec((tm, d), lambda i: (i, 0)),
                  pl.BlockSpec((tm, dc), lambda i: (i, 0)),
                  pl.BlockSpec((tm, d - dc), lambda i: (i, 0)),
                  pl.BlockSpec((dc, d), lambda i: (0, 0)),
                  pl.BlockSpec((d - dc, d), lambda i: (0, 0)),
                  pl.BlockSpec((1, d), lambda i: (0, 0)),
                  pl.BlockSpec((d, nq), lambda i: (0, 0)),
                  pl.BlockSpec((hp, n_keys, kd), lambda i: (0, 0, 0))],
        out_specs=[pl.BlockSpec((tm, d), lambda i: (i, 0)),
                   pl.BlockSpec((tm, d), lambda i: (i, 0)),
                   pl.BlockSpec((heads, k, tm), lambda i: (0, 0, i)),
                   pl.BlockSpec((heads, k, tm), lambda i: (0, 0, i))],
        out_shape=[jax.ShapeDtypeStruct((t, d), F32),
                   jax.ShapeDtypeStruct((t, d), F32),
                   jax.ShapeDtypeStruct((heads, k, t), jnp.int32),
                   jax.ShapeDtypeStruct((heads, k, t), F32)],
        scratch_shapes=[pltpu.VMEM((tm, nq), F32)],
        compiler_params=_cparams(("arbitrary",)),
        name="mid",
    )(x2, ya, yb, wo_bf[:dc], wo_bf[dc:], g, wq_bf, keys_bf)


def _peer_consts(picks):
    col = np.arange(picks * SUBLANES)
    diag = (col[None, :] % SUBLANES == np.arange(SUBLANES)[:, None]).astype(np.float32)
    grp = (col[:, None] // SUBLANES == np.arange(picks)[None, :]).astype(np.float32)
    return jnp.asarray(diag), jnp.asarray(grp, BF16), jnp.asarray(grp.T, BF16)


_SLAB = SUBLANES // 2
_PEER_UNROLL = 4


def _gather_rows(eidx_ref, tab_ref, gb_ref, t, picks):
    for j in range(picks):
        row = pl.multiple_of(eidx_ref[t, j], _SLAB)
        gb_ref[j * _SLAB:(j + 1) * _SLAB, :] = tab_ref[pl.ds(row, _SLAB), :]


def _token_pipeline(tb, gather, compute):
    u_n = _PEER_UNROLL
    for u in range(u_n):
        gather(u, u)

    def two_steps(i, carry):
        for half in range(2):
            first = (2 * i + half) * u_n
            for u in range(u_n):
                gather(jnp.minimum(first + u_n + u, tb - 1), (1 - half) * u_n + u)
            for u in range(u_n):
                compute(first + u, half * u_n + u)
        return carry

    lax.fori_loop(0, tb // (2 * u_n), two_steps, 0)


def _peer_down_kernel(eidx_ref, x_ref, gate_ref, tab_ref, diag_ref, grp_ref, w_ref, gb_ref, z_ref):
    tb, picks = gate_ref.shape

    def gather(t, slot):
        _gather_rows(eidx_ref, tab_ref, gb_ref.at[slot], t, picks)

    def compute(t, slot):
        w = pltpu.bitcast(gb_ref[slot], BF16)
        xhi, xlo = _split2(x_ref[t])
        part = (lax.dot_general(xhi, w, _NT, preferred_element_type=F32)
                + lax.dot_general(xlo, w, _NT, preferred_element_type=F32))
        z_ref[t] = part * diag_ref[...]

    _token_pipeline(tb, gather, compute)
    z = z_ref[...].reshape(tb * SUBLANES, picks * SUBLANES)
    act = jnp.sum(_dot2(z, grp_ref[...]).reshape(tb, SUBLANES, picks), axis=1)
    gelu = 0.5 * act * (1.0 + lax.erf(act * np.float32(np.sqrt(0.5))))
    w_ref[...] = gate_ref[...] * gelu


def _peer_up_kernel(eidx_ref, w_in_ref, h2_ref, g_ref, tab_ref, diag_ref, grpt_ref, o_ref,
                    gb_ref, wx_ref, *, d):
    tb, picks = w_in_ref.shape
    wx_ref[...] = _dot2(w_in_ref[...], grpt_ref[...])

    def gather(t, slot):
        _gather_rows(eidx_ref, tab_ref, gb_ref.at[slot], t, picks)

    def compute(t, slot):
        v = pltpu.bitcast(gb_ref[slot], BF16)
        coef = wx_ref[pl.ds(t, 1), :] * diag_ref[...]
        chi, clo = _split2(coef)
        y = (jnp.dot(chi, v, preferred_element_type=F32)
             + jnp.dot(clo, v, preferred_element_type=F32))
        o_ref[t] = h2_ref[t] + y

    _token_pipeline(tb, gather, compute)
    h = o_ref[...]
    ms = jnp.sum(jnp.sum(h * h, axis=2, keepdims=True), axis=1, keepdims=True) / d
    o_ref[...] = h * lax.rsqrt(ms + EPS) * g_ref[...]


def _table_spec(tab):
    return pl.BlockSpec(tab.shape, lambda i: (0, 0), pipeline_mode=pl.Buffered(1))


def _pack_table(tab):
    n = tab.shape[0]
    pairs = tab.astype(BF16).reshape(n * _SLAB, 2, LANES).transpose(0, 2, 1)
    return lax.bitcast_convert_type(pairs, jnp.int32)


def _peer(xn2, h2, eidx_t, gate_t, u_tab, v_tab, final_g, tb=128):
    t, d = xn2.shape
    picks = eidx_t.shape[1]
    assert d == SUBLANES * LANES and tb % (2 * _PEER_UNROLL) == 0
    diag, grp, grpt = _peer_consts(picks)
    eidx_t = eidx_t * _SLAB
    x8 = xn2.reshape(t, SUBLANES, LANES)
    h8 = h2.reshape(t, SUBLANES, LANES)
    g8 = final_g.reshape(1, SUBLANES, LANES).astype(F32)
    smem_idx = pl.BlockSpec((tb, picks), lambda i: (i, 0), memory_space=pltpu.SMEM)
    row3 = pl.BlockSpec((tb, SUBLANES, LANES), lambda i: (i, 0, 0))
    row2 = pl.BlockSpec((tb, picks), lambda i: (i, 0))
    const2 = lambda a: pl.BlockSpec(a.shape, lambda i: (0, 0))

    w = pl.pallas_call(
        _peer_down_kernel,
        grid=(t // tb,),
        in_specs=[smem_idx, row3, row2, _table_spec(u_tab), const2(diag), const2(grp)],
        out_specs=row2,
        out_shape=jax.ShapeDtypeStruct((t, picks), F32),
        scratch_shapes=[pltpu.VMEM((2 * _PEER_UNROLL, picks * _SLAB, LANES), jnp.int32),
                        pltpu.VMEM((tb, SUBLANES, picks * SUBLANES), F32)],
        compiler_params=_cparams(("arbitrary",)),
        name="peer_down",
    )(eidx_t, x8, gate_t, u_tab, diag, grp)

    out = pl.pallas_call(
        functools.partial(_peer_up_kernel, d=d),
        grid=(t // tb,),
        in_specs=[smem_idx, row2, row3, pl.BlockSpec((1, SUBLANES, LANES), lambda i: (0, 0, 0)),
                  _table_spec(v_tab), const2(diag), const2(grpt)],
        out_specs=row3,
        out_shape=jax.ShapeDtypeStruct((t, SUBLANES, LANES), F32),
        scratch_shapes=[pltpu.VMEM((2 * _PEER_UNROLL, picks * _SLAB, LANES), jnp.int32),
                        pltpu.VMEM((tb, picks * SUBLANES), F32)],
        compiler_params=_cparams(("arbitrary",)),
        name="peer_up",
    )(eidx_t, w, h8, g8, v_tab, diag, grpt)
    return out.reshape(t, d)


def kernel(x, norm_mix_g, w_in, conv_w, conv_b, conv_norm_g, conv_norm_b, hgrn_lb_logits,
           hgrn_norm_g, w_out, norm_ffn_g, peer_w_query, peer_sub_keys, peer_u, peer_v,
           final_norm_g):
    bsz, s, d = x.shape
    depth = w_in.shape[0]
    dc = conv_w.shape[2]
    dh = hgrn_norm_g.shape[1]
    heads_h = dh // LANES
    ph, _, n_keys, kd = peer_sub_keys.shape[1:]
    t = bsz * s
    row = lambda a: a.reshape(1, -1).astype(F32)

    lb_all = jnp.cumsum(jax.nn.softmax(hgrn_lb_logits.astype(F32), axis=0), axis=0)
    h = x.reshape(t, d)
    for l in range(depth):
        proj = _inproj(h, row(norm_mix_g[l]), w_in[l].astype(BF16))
        proj3 = proj.reshape(bsz, s, -1)
        ya = _conv(proj3, conv_w[l], row(conv_b[l]), row(conv_norm_g[l]), row(conv_norm_b[l]))
        yb = _hgrn(proj3, row(lb_all[l]), row(hgrn_norm_g[l]), heads_h)
        h2, xn2, eidx, gate = _mid(h, ya.reshape(t, dc), yb.reshape(t, dh), w_out[l].astype(BF16),
                                   row(norm_ffn_g[l]), peer_w_query[l].astype(BF16),
                                   peer_sub_keys[l].reshape(ph * 2, n_keys, kd).astype(BF16))
        picks = ph * PEER_TOPK
        eidx_t = eidx.reshape(picks, t).T
        gate_t = gate.reshape(picks, t).T
        u_tab = _pack_table(peer_u[l])
        v_tab = _pack_table(peer_v[l])
        last = l == depth - 1
        g_out = final_norm_g if last else jnp.ones((d,), F32)
        h = _peer(xn2, h2, eidx_t, gate_t, u_tab, v_tab, g_out)
        assert last, "a deeper stack needs the un-normalised residual between layers"
    return h.reshape(bsz, s, d)
```

```python
import functools

import numpy as np
import jax
import jax.numpy as jnp
from jax import lax
from jax.experimental import pallas as pl
from jax.experimental.pallas import tpu as pltpu

F32 = jnp.float32
BF16 = jnp.bfloat16
EPS = 1e-6

LANES = 128
SUBLANES = 8
_SLAB = SUBLANES // 2
CONV_GROUPS = 8
HGRN_CHUNK = 64
PEER_TOPK = 16
VMEM_LIMIT = 48 * 1024 * 1024

_NT = (((1,), (1,)), ((), ()))
_TN = (((0,), (0,)), ((), ()))


def _cparams(sem):
    return pltpu.CompilerParams(dimension_semantics=sem, vmem_limit_bytes=VMEM_LIMIT)


def _split2(x):
    hi = x.astype(BF16)
    lo = (x - hi.astype(F32)).astype(BF16)
    return hi, lo


def _dot2(x, w):
    hi, lo = _split2(x)
    return (jnp.dot(hi, w, preferred_element_type=F32)
            + jnp.dot(lo, w, preferred_element_type=F32))


def _inproj_kernel(x_ref, g_ref, w_ref, o_ref):
    x = x_ref[...]
    xn = x * lax.rsqrt(jnp.mean(x * x, axis=-1, keepdims=True) + EPS) * g_ref[...]
    o_ref[...] = jnp.dot(xn.astype(BF16), w_ref[...], preferred_element_type=F32)


def _inproj(x2, g, w_bf, tm=256):
    t, d = x2.shape
    n = w_bf.shape[1]
    return pl.pallas_call(
        _inproj_kernel,
        grid=(t // tm,),
        in_specs=[pl.BlockSpec((tm, d), lambda i: (i, 0)),
                  pl.BlockSpec((1, d), lambda i: (0, 0)),
                  pl.BlockSpec((d, n), lambda i: (0, 0))],
        out_specs=pl.BlockSpec((tm, n), lambda i: (i, 0)),
        out_shape=jax.ShapeDtypeStruct((t, n), F32),
        compiler_params=_cparams(("arbitrary",)),
        name="inproj",
    )(x2, g, w_bf)


_CONV_SUB = 32
_CONV_HIST = 32


def _conv_kernel(p_ref, w_ref, cb_ref, ng_ref, nb_ref, gm_ref, o_ref, hbuf, pre, *, width):
    tm, dc = pre.shape
    j = pl.program_id(1)

    @pl.when(j == 0)
    def _():
        hbuf[0:_CONV_HIST, :] = jnp.zeros((_CONV_HIST, dc), F32)

    @pl.when(j > 0)
    def _():
        hbuf[0:_CONV_HIST, :] = hbuf[tm:tm + _CONV_HIST, :]

    a = p_ref[:, 0:dc]
    gate = p_ref[:, dc:2 * dc]
    hbuf[_CONV_HIST:_CONV_HIST + tm, :] = a * jax.nn.sigmoid(gate)

    first = _CONV_HIST - (width - 1)

    def body(r, carry):
        base = pl.multiple_of(r * _CONV_SUB, _CONV_SUB)
        win = hbuf[pl.ds(base, 2 * _CONV_SUB), :]
        acc = jnp.broadcast_to(cb_ref[...], (_CONV_SUB, dc))
        for ph in range(SUBLANES):
            offs = [first + k for k in range(width) if (first + k) % SUBLANES == ph]
            if not offs:
                continue
            span = max(offs) - ph + _CONV_SUB
            sh = win[ph:ph + span, :]
            for o in offs:
                k = o - first
                acc = acc + sh[o - ph:o - ph + _CONV_SUB, :] * w_ref[k:k + 1, :]
        pre[pl.ds(base, _CONV_SUB), :] = acc
        return carry

    lax.fori_loop(0, tm // _CONV_SUB, body, 0)

    h = pre[...]
    gm = gm_ref[...]
    mu = _dot2(h, gm)
    d = h - mu
    var = _dot2(d * d, gm)
    hn = d * lax.rsqrt(var + EPS) * ng_ref[...] + nb_ref[...]
    o_ref[...] = (hn * jax.nn.sigmoid(hn)).astype(o_ref.dtype)


def _conv(proj3, conv_w, conv_b, norm_g, norm_b, tm=512):
    b, s, _ = proj3.shape
    width, dc = conv_w.shape
    assert width - 1 <= _CONV_HIST and tm % _CONV_SUB == 0 and s % tm == 0
    gsz = dc // CONV_GROUPS
    gid = np.arange(dc) // gsz
    gm = jnp.asarray((gid[:, None] == gid[None, :]).astype(np.float32) / gsz, BF16)
    return pl.pallas_call(
        functools.partial(_conv_kernel, width=width),
        grid=(b, s // tm),
        in_specs=[pl.BlockSpec((None, tm, 2 * dc), lambda i, j: (i, j, 0)),
                  pl.BlockSpec((width, dc), lambda i, j: (0, 0)),
                  pl.BlockSpec((1, dc), lambda i, j: (0, 0)),
                  pl.BlockSpec((1, dc), lambda i, j: (0, 0)),
                  pl.BlockSpec((1, dc), lambda i, j: (0, 0)),
                  pl.BlockSpec((dc, dc), lambda i, j: (0, 0))],
        out_specs=pl.BlockSpec((None, tm, dc), lambda i, j: (i, j, 0)),
        out_shape=jax.ShapeDtypeStruct((b, s, dc), BF16),
        scratch_shapes=[pltpu.VMEM((tm + _CONV_HIST, dc), F32),
                        pltpu.VMEM((tm, dc), F32)],
        compiler_params=_cparams(("arbitrary", "arbitrary")),
        name="conformer_conv",
    )(proj3, conv_w, conv_b, norm_g, norm_b, gm)


def _hgrn_tables(c):
    r = np.arange(c)
    mats = [(r[None, :] <= r[:, None]), (r[None, :] > r[:, None])]
    masks = []
    m = 1
    while m < c:
        blk, pos = r // (2 * m), r % (2 * m)
        anchor = blk * 2 * m + m - 1
        second = pos >= m
        p = np.where(second[:, None],
                     (r[None, :] > anchor[:, None]) & (r[None, :] <= r[:, None]),
                     (r[None, :] > r[:, None]) & (r[None, :] <= anchor[:, None]))
        mats.append(p)
        masks.append((blk[:, None] == blk[None, :]) & second[:, None] & (~second)[None, :])
        m *= 2
    pm = np.concatenate(mats, axis=0).astype(np.float32)
    return jnp.asarray(pm, BF16), jnp.asarray(np.stack(masks).astype(np.float32))


def _hgrn_kernel(qf_ref, ig_ref, lb_ref, ng_ref, pm_ref, mask_ref, o_ref, st_ref, *, heads):
    @pl.when(pl.program_id(1) == 0)
    def _():
        st_ref[...] = jnp.zeros(st_ref.shape, F32)

    for bi in range(o_ref.shape[0]):
        _hgrn_chunk(qf_ref.at[bi], ig_ref.at[bi], lb_ref, ng_ref, pm_ref, mask_ref,
                    o_ref.at[bi], st_ref.at[bi], heads)


def _hgrn_chunk(qf_ref, ig_ref, lb_ref, ng_ref, pm_ref, mask_ref, o_ref, st_ref, heads):
    c, dh = o_ref.shape
    hd = dh // heads
    nlev = mask_ref.shape[0]

    lb = lb_ref[...]
    qin = qf_ref[:, 0:dh]
    z = qf_ref[:, dh:2 * dh]
    vin = ig_ref[:, 0:dh]
    gin = ig_ref[:, dh:2 * dh]

    f = lb + (1.0 - lb) * jax.nn.sigmoid(z)
    logf = jnp.log(f)
    kk = (1.0 - lb) * jax.nn.sigmoid(-z)
    q = qin * jax.nn.sigmoid(qin)

    pm = pm_ref[...]
    hi = logf.astype(BF16)
    r1 = logf - hi.astype(F32)
    mid = r1.astype(BF16)
    lo = (r1 - mid.astype(F32)).astype(BF16)
    ex = (jnp.dot(pm, hi, preferred_element_type=F32)
          + jnp.dot(pm, mid, preferred_element_type=F32)
          + jnp.dot(pm, lo, preferred_element_type=F32))
    b = ex[0:c]
    suf = ex[c:2 * c]

    for h in range(heads):
        sl = slice(h * hd, (h + 1) * hd)
        qh, kh, vh = q[:, sl], kk[:, sl], vin[:, sl]
        vb = vh.astype(BF16)
        st = st_ref[h]
        qb = (qh * jnp.exp(b[:, sl])).astype(BF16)
        o = lax.dot_general(qb, st.astype(BF16), _NT, preferred_element_type=F32)
        scores = jnp.zeros((c, c), F32)
        for lv in range(nlev):
            el = jnp.exp(ex[(2 + lv) * c:(3 + lv) * c, sl])
            s = lax.dot_general((qh * el).astype(BF16), (kh * el).astype(BF16), _NT,
                                preferred_element_type=F32)
            scores = scores + mask_ref[lv] * s
        o = o + jnp.dot(scores.astype(BF16), vb, preferred_element_type=F32)
        o = o + jnp.sum(qh * kh, axis=-1, keepdims=True) * vh
        kd = (kh * jnp.exp(suf[:, sl])).astype(BF16)
        st_ref[h] = (st * jnp.exp(b[c - 1:c, sl])
                     + lax.dot_general(vb, kd, _TN, preferred_element_type=F32))
        o = o * lax.rsqrt(jnp.mean(o * o, axis=-1, keepdims=True) + EPS) * ng_ref[:, sl]
        gh = gin[:, sl]
        o_ref[:, sl] = (o * (gh * jax.nn.sigmoid(gh))).astype(o_ref.dtype)


def _hgrn(proj3, lb, norm_g, heads, c=HGRN_CHUNK, nb=4):
    b, s, n = proj3.shape
    dh = lb.shape[1]
    nblk = n // (2 * dh)
    nb = min(nb, b)
    assert b % nb == 0
    pm, masks = _hgrn_tables(c)
    return pl.pallas_call(
        functools.partial(_hgrn_kernel, heads=heads),
        grid=(b // nb, s // c),
        in_specs=[pl.BlockSpec((nb, c, 2 * dh), lambda i, j: (i, j, nblk - 2)),
                  pl.BlockSpec((nb, c, 2 * dh), lambda i, j: (i, j, nblk - 1)),
                  pl.BlockSpec((1, dh), lambda i, j: (0, 0)),
                  pl.BlockSpec((1, dh), lambda i, j: (0, 0)),
                  pl.BlockSpec(pm.shape, lambda i, j: (0, 0)),
                  pl.BlockSpec(masks.shape, lambda i, j: (0, 0, 0))],
        out_specs=pl.BlockSpec((nb, c, dh), lambda i, j: (i, j, 0)),
        out_shape=jax.ShapeDtypeStruct((b, s, dh), BF16),
        scratch_shapes=[pltpu.VMEM((nb, heads, dh // heads, dh // heads), F32)],
        compiler_params=_cparams(("arbitrary", "arbitrary")),
        name="hgrn2",
    )(proj3, proj3, lb, norm_g, pm, masks)


def _sort_network(n):
    pairs = []

    def merge(lo, cnt, r):
        step = 2 * r
        if step < cnt:
            merge(lo, cnt, step)
            merge(lo + r, cnt, step)
            pairs.extend((i, i + r) for i in range(lo + r, lo + cnt - r, step))
        else:
            pairs.append((lo, lo + r))

    def sort(lo, cnt):
        if cnt > 1:
            sort(lo, cnt // 2)
            sort(lo + cnt // 2, cnt // 2)
            merge(lo, cnt, 1)

    sort(0, n)
    return pairs


def _sort_blocks(vs, ids):
    vs, ids = list(vs), list(ids)
    for i, j in _sort_network(len(vs)):
        first = vs[i] >= vs[j]
        vs[i], vs[j] = jnp.maximum(vs[i], vs[j]), jnp.minimum(vs[i], vs[j])
        ids[i], ids[j] = jnp.where(first, ids[i], ids[j]), jnp.where(first, ids[j], ids[i])
    return vs, ids


def _merge_topk(vs, ids, k):
    vs, ids = list(vs), list(ids)
    out_v, out_i = [], []
    for r in range(k):
        mx = jnp.broadcast_to(jnp.max(vs[0], axis=0, keepdims=True), vs[0].shape)
        pick = jnp.where(vs[0] == mx, ids[0], jnp.int32(2 ** 30))
        pick = jnp.broadcast_to(jnp.min(pick, axis=0, keepdims=True), pick.shape)
        out_v.append(mx)
        out_i.append(pick)
        hit = ids[0] == pick
        for j in range(min(k - r - 1, len(vs) - 1)):
            vs[j] = jnp.where(hit, vs[j + 1], vs[j])
            ids[j] = jnp.where(hit, ids[j + 1], ids[j])
    return out_v, out_i


def _rows_from_blocks(blocks):
    sub = lax.broadcasted_iota(jnp.int32, blocks[0].shape, 0)
    out = []
    for g in range(0, len(blocks), SUBLANES):
        acc = blocks[g]
        for r in range(1, SUBLANES):
            acc = jnp.where(sub == r, blocks[g + r], acc)
        out.append(acc)
    return jnp.concatenate(out, axis=0)


def _mid_kernel(x_ref, ya_ref, yb_ref, woa_ref, wob_ref, g_ref, wq_ref, keys_ref,
                h2_ref, xn8_ref, eidx_ref, gate_ref, q_scr, e_scr, g_scr, *, n_keys):
    k = PEER_TOPK
    kq = int(np.sqrt(k))
    tm, d = h2_ref.shape
    heads = keys_ref.shape[0] // 2
    kd = keys_ref.shape[2]
    nblk = n_keys // SUBLANES

    h2 = (x_ref[...]
          + jnp.dot(ya_ref[...], woa_ref[...], preferred_element_type=F32)
          + jnp.dot(yb_ref[...], wob_ref[...], preferred_element_type=F32))
    h2_ref[...] = h2
    xn = h2 * lax.rsqrt(jnp.mean(h2 * h2, axis=-1, keepdims=True) + EPS) * g_ref[...]
    for c in range(d // LANES):
        xn8_ref[pl.ds(c, tm, stride=d // LANES), :] = xn[:, c * LANES:(c + 1) * LANES]
    q_scr[...] = jnp.dot(xn.astype(BF16), wq_ref[...], preferred_element_type=F32)

    sub = lax.broadcasted_iota(jnp.int32, (SUBLANES, tm), 0)
    low = sub < kq

    def head(h, carry):
        tops = []
        for p in range(2):
            col = pl.multiple_of((2 * h + p) * kd, kd)
            qhp = q_scr[:, pl.ds(col, kd)].astype(BF16)
            sc = lax.dot_general(keys_ref[2 * h + p], qhp, _NT,
                                 preferred_element_type=F32)
            vs = [sc[i * SUBLANES:(i + 1) * SUBLANES] for i in range(nblk)]
            ids = [sub + i * SUBLANES for i in range(nblk)]
            tops.append(_merge_topk(*_sort_blocks(vs, ids), k))
        (v1, i1), (v2, i2) = tops
        colv, coli = v1[0], i1[0]
        for s_ in range(1, 2 * kq):
            src_v, src_i = (v1[s_], i1[s_]) if s_ < kq else (v2[s_ - kq], i2[s_ - kq])
            colv = jnp.where(sub == s_, src_v, colv)
            coli = jnp.where(sub == s_, src_i, coli)
        cand, cid = [], []
        for j in range(k):
            if j + kq < k:
                cand.append(colv + jnp.where(low, v2[j], v1[j + kq]))
                cid.append(jnp.where(low, coli * n_keys + i2[j], i1[j + kq] * n_keys + coli))
            else:
                cand.append(jnp.where(low, colv + v2[j], -jnp.inf))
                cid.append(jnp.where(low, coli * n_keys + i2[j], -1))
        tv, te = _merge_topk(cand, cid, k)
        ez = [jnp.exp(v - tv[0]) for v in tv]
        den = ez[0]
        for e_ in ez[1:]:
            den = den + e_
        row0 = pl.multiple_of(h * k, k)
        e_scr[pl.ds(row0, k), :] = _rows_from_blocks(te) * _SLAB
        g_scr[pl.ds(row0, k), :] = _rows_from_blocks([e_ / den for e_ in ez])
        return carry

    lax.fori_loop(0, heads, head, 0)
    eidx_ref[...] = e_scr[...].T
    gate_ref[...] = g_scr[...].T


def _mid(x2, ya, yb, wo_bf, g, wq_bf, keys_bf, tm=256):
    t, d = x2.shape
    dc = ya.shape[1]
    hp, n_keys, kd = keys_bf.shape
    heads = hp // 2
    nq = wq_bf.shape[1]
    picks = heads * PEER_TOPK
    assert d == SUBLANES * LANES and 2 * int(np.sqrt(PEER_TOPK)) == SUBLANES
    return pl.pallas_call(
        functools.partial(_mid_kernel, n_keys=n_keys),
        grid=(t // tm,),
        in_specs=[pl.BlockSpec((tm, d), lambda i: (i, 0)),
                  pl.BlockSpec((tm, dc), lambda i: (i, 0)),
                  pl.BlockSpec((tm, d - dc), lambda i: (i, 0)),
                  pl.BlockSpec((dc, d), lambda i: (0, 0)),
                  pl.BlockSpec((d - dc, d), lambda i: (0, 0)),
                  pl.BlockSpec((1, d), lambda i: (0, 0)),
                  pl.BlockSpec((d, nq), lambda i: (0, 0)),
                  pl.BlockSpec((hp, n_keys, kd), lambda i: (0, 0, 0))],
        out_specs=[pl.BlockSpec((tm, d), lambda i: (i, 0)),
                   pl.BlockSpec((tm * SUBLANES, LANES), lambda i: (i, 0)),
                   pl.BlockSpec((tm, picks), lambda i: (i, 0)),
                   pl.BlockSpec((tm, picks), lambda i: (i, 0))],
        out_shape=[jax.ShapeDtypeStruct((t, d), F32),
                   jax.ShapeDtypeStruct((t * SUBLANES, LANES), F32),
                   jax.ShapeDtypeStruct((t, picks), jnp.int32),
                   jax.ShapeDtypeStruct((t, picks), F32)],
        scratch_shapes=[pltpu.VMEM((tm, nq), F32),
                        pltpu.VMEM((picks, tm), jnp.int32),
                        pltpu.VMEM((picks, tm), F32)],
        compiler_params=_cparams(("arbitrary",)),
        name="mid",
    )(x2, ya, yb, wo_bf[:dc], wo_bf[dc:], g, wq_bf, keys_bf)


def _peer_consts(picks):
    col = np.arange(picks * SUBLANES)
    diag = (col[None, :] % SUBLANES == np.arange(SUBLANES)[:, None]).astype(np.float32)
    grp = (col[:, None] // SUBLANES == np.arange(picks)[None, :]).astype(np.float32)
    return jnp.asarray(diag), jnp.asarray(grp, BF16), jnp.asarray(grp.T, BF16)


_PEER_UNROLL = 8


def _gather_rows(eidx_ref, tab_ref, gb_ref, t, picks):
    for j in range(picks):
        row = pl.multiple_of(eidx_ref[t, j], _SLAB)
        gb_ref[j * _SLAB:(j + 1) * _SLAB, :] = tab_ref[pl.ds(row, _SLAB), :]


def _token_pipeline(tb, gather, compute):
    u_n = _PEER_UNROLL
    for u in range(u_n):
        gather(u, u)

    def two_steps(i, carry):
        for half in range(2):
            first = (2 * i + half) * u_n
            for u in range(u_n):
                gather(jnp.minimum(first + u_n + u, tb - 1), (1 - half) * u_n + u)
            for u in range(u_n):
                compute(first + u, half * u_n + u)
        return carry

    lax.fori_loop(0, tb // (2 * u_n), two_steps, 0)


def _peer_down_kernel(eidx_ref, x_ref, gate_ref, tab_ref, diag_ref, grp_ref, w_ref, gb_ref, z_ref):
    tb, picks = gate_ref.shape

    def gather(t, slot):
        _gather_rows(eidx_ref, tab_ref, gb_ref.at[slot], t, picks)

    def compute(t, slot):
        w = pltpu.bitcast(gb_ref[slot], BF16)
        xhi, xlo = _split2(x_ref[t])
        part = (lax.dot_general(xhi, w, _NT, preferred_element_type=F32)
                + lax.dot_general(xlo, w, _NT, preferred_element_type=F32))
        z_ref[t] = part * diag_ref[...]

    _token_pipeline(tb, gather, compute)
    z = z_ref[...].reshape(tb * SUBLANES, picks * SUBLANES)
    act = jnp.sum(_dot2(z, grp_ref[...]).reshape(tb, SUBLANES, picks), axis=1)
    gelu = 0.5 * act * (1.0 + lax.erf(act * np.float32(np.sqrt(0.5))))
    w_ref[...] = gate_ref[...] * gelu


def _peer_up_kernel(eidx_ref, w_in_ref, h2_ref, g_ref, tab_ref, diag_ref, grpt_ref, o_ref,
                    gb_ref, wx_ref, y_ref):
    tb, picks = w_in_ref.shape
    d = o_ref.shape[1]
    wx_ref[...] = _dot2(w_in_ref[...], grpt_ref[...])

    def gather(t, slot):
        _gather_rows(eidx_ref, tab_ref, gb_ref.at[slot], t, picks)

    def compute(t, slot):
        v = pltpu.bitcast(gb_ref[slot], BF16)
        coef = wx_ref[pl.ds(t, 1), :] * diag_ref[...]
        chi, clo = _split2(coef)
        y_ref[pl.ds(pl.multiple_of(t * SUBLANES, SUBLANES), SUBLANES), :] = (
            jnp.dot(chi, v, preferred_element_type=F32)
            + jnp.dot(clo, v, preferred_element_type=F32))

    _token_pipeline(tb, gather, compute)
    ms = jnp.zeros((tb, 1), F32)
    for c in range(SUBLANES):
        cols = slice(c * LANES, (c + 1) * LANES)
        hc = h2_ref[:, cols] + y_ref[pl.ds(c, tb, stride=SUBLANES), :]
        ms = ms + jnp.sum(hc * hc, axis=-1, keepdims=True)
        o_ref[:, cols] = hc
    o_ref[...] = o_ref[...] * lax.rsqrt(ms / d + EPS) * g_ref[...]


def _table_spec(tab):
    return pl.BlockSpec(tab.shape, lambda i: (0, 0), pipeline_mode=pl.Buffered(1))


def _pack_kernel(t_ref, o_ref, z_ref):
    tr = t_ref.shape[0]
    for c in range(SUBLANES):
        z_ref[pl.ds(c, tr, stride=SUBLANES), :] = t_ref[:, c * LANES:(c + 1) * LANES]
    o_ref[...] = pltpu.bitcast(z_ref[...].astype(BF16), jnp.int32)


def _pack_table(tab, tr=256):
    n, d = tab.shape
    assert d == SUBLANES * LANES and n % tr == 0
    return pl.pallas_call(
        _pack_kernel,
        grid=(n // tr,),
        in_specs=[pl.BlockSpec((tr, d), lambda i: (i, 0))],
        out_specs=pl.BlockSpec((tr * _SLAB, LANES), lambda i: (i, 0)),
        out_shape=jax.ShapeDtypeStruct((n * _SLAB, LANES), jnp.int32),
        scratch_shapes=[pltpu.VMEM((tr * SUBLANES, LANES), F32)],
        compiler_params=_cparams(("arbitrary",)),
        name="pack_table",
    )(tab)


def _peer(xn8, h2, eidx_t, gate_t, u_tab, v_tab, final_g, tb=128):
    t, d = h2.shape
    picks = eidx_t.shape[1]
    assert d == SUBLANES * LANES and tb % (2 * _PEER_UNROLL) == 0
    diag, grp, grpt = _peer_consts(picks)
    x8 = xn8.reshape(t, SUBLANES, LANES)
    smem_idx = pl.BlockSpec((tb, picks), lambda i: (i, 0), memory_space=pltpu.SMEM)
    row3 = pl.BlockSpec((tb, SUBLANES, LANES), lambda i: (i, 0, 0))
    row2 = pl.BlockSpec((tb, picks), lambda i: (i, 0))
    rowd = pl.BlockSpec((tb, d), lambda i: (i, 0))
    const2 = lambda a: pl.BlockSpec(a.shape, lambda i: (0, 0))

    w = pl.pallas_call(
        _peer_down_kernel,
        grid=(t // tb,),
        in_specs=[smem_idx, row3, row2, _table_spec(u_tab), const2(diag), const2(grp)],
        out_specs=row2,
        out_shape=jax.ShapeDtypeStruct((t, picks), F32),
        scratch_shapes=[pltpu.VMEM((2 * _PEER_UNROLL, picks * _SLAB, LANES), jnp.int32),
                        pltpu.VMEM((tb, SUBLANES, picks * SUBLANES), F32)],
        compiler_params=_cparams(("arbitrary",)),
        name="peer_down",
    )(eidx_t, x8, gate_t, u_tab, diag, grp)

    return pl.pallas_call(
        _peer_up_kernel,
        grid=(t // tb,),
        in_specs=[smem_idx, row2, rowd, const2(final_g), _table_spec(v_tab), const2(diag),
                  const2(grpt)],
        out_specs=rowd,
        out_shape=jax.ShapeDtypeStruct((t, d), F32),
        scratch_shapes=[pltpu.VMEM((2 * _PEER_UNROLL, picks * _SLAB, LANES), jnp.int32),
                        pltpu.VMEM((tb, picks * SUBLANES), F32),
                        pltpu.VMEM((tb * SUBLANES, LANES), F32)],
        compiler_params=_cparams(("arbitrary",)),
        name="peer_up",
    )(eidx_t, w, h2, final_g, v_tab, diag, grpt)


def kernel(x, norm_mix_g, w_in, conv_w, conv_b, conv_norm_g, conv_norm_b, hgrn_lb_logits,
           hgrn_norm_g, w_out, norm_ffn_g, peer_w_query, peer_sub_keys, peer_u, peer_v,
           final_norm_g):
    bsz, s, d = x.shape
    depth = w_in.shape[0]
    dc = conv_w.shape[2]
    dh = hgrn_norm_g.shape[1]
    heads_h = dh // LANES
    ph, _, n_keys, kd = peer_sub_keys.shape[1:]
    t = bsz * s
    row = lambda a: a.reshape(1, -1).astype(F32)

    lb_all = jnp.cumsum(jax.nn.softmax(hgrn_lb_logits.astype(F32), axis=0), axis=0)
    h = x.reshape(t, d)
    for l in range(depth):
        proj = _inproj(h, row(norm_mix_g[l]), w_in[l].astype(BF16))
        proj3 = proj.reshape(bsz, s, -1)
        ya = _conv(proj3, conv_w[l], row(conv_b[l]), row(conv_norm_g[l]), row(conv_norm_b[l]))
        yb = _hgrn(proj3, row(lb_all[l]), row(hgrn_norm_g[l]), heads_h)
        h2, xn8, offs, gate = _mid(h, ya.reshape(t, dc), yb.reshape(t, dh), w_out[l].astype(BF16),
                                   row(norm_ffn_g[l]), peer_w_query[l].astype(BF16),
                                   peer_sub_keys[l].reshape(ph * 2, n_keys, kd).astype(BF16))
        assert l == depth - 1, "a deeper stack needs the un-normalised residual between layers"
        h = _peer(xn8, h2, offs, gate, _pack_table(peer_u[l]), _pack_table(peer_v[l]),
                  row(final_norm_g))
    return h.reshape(bsz, s, d)
```

```python
import functools

import numpy as np
import jax
import jax.numpy as jnp
from jax import lax
from jax.experimental import pallas as pl
from jax.experimental.pallas import tpu as pltpu

F32 = jnp.float32
BF16 = jnp.bfloat16
EPS = 1e-6

LANES = 128
SUBLANES = 8
_SLAB = SUBLANES // 2
CONV_GROUPS = 8
HGRN_CHUNK = 64
PEER_TOPK = 16
VMEM_LIMIT = 48 * 1024 * 1024

_NT = (((1,), (1,)), ((), ()))
_TN = (((0,), (0,)), ((), ()))


def _cparams(sem):
    return pltpu.CompilerParams(dimension_semantics=sem, vmem_limit_bytes=VMEM_LIMIT)


def _split2(x):
    hi = x.astype(BF16)
    lo = (x - hi.astype(F32)).astype(BF16)
    return hi, lo


def _dot2(x, w):
    hi, lo = _split2(x)
    return (jnp.dot(hi, w, preferred_element_type=F32)
            + jnp.dot(lo, w, preferred_element_type=F32))


def _inproj_kernel(x_ref, g_ref, w_ref, o_ref):
    x = x_ref[...]
    xn = x * lax.rsqrt(jnp.mean(x * x, axis=-1, keepdims=True) + EPS) * g_ref[...]
    o_ref[...] = jnp.dot(xn.astype(BF16), w_ref[...], preferred_element_type=F32)


def _inproj(x2, g, w_bf, tm=256):
    t, d = x2.shape
    n = w_bf.shape[1]
    return pl.pallas_call(
        _inproj_kernel,
        grid=(t // tm,),
        in_specs=[pl.BlockSpec((tm, d), lambda i: (i, 0)),
                  pl.BlockSpec((1, d), lambda i: (0, 0)),
                  pl.BlockSpec((d, n), lambda i: (0, 0))],
        out_specs=pl.BlockSpec((tm, n), lambda i: (i, 0)),
        out_shape=jax.ShapeDtypeStruct((t, n), F32),
        compiler_params=_cparams(("arbitrary",)),
        name="inproj",
    )(x2, g, w_bf)


_CONV_SUB = 32
_CONV_HIST = 32


def _conv_kernel(p_ref, w_ref, cb_ref, ng_ref, nb_ref, gm_ref, o_ref, hbuf, pre, *, width):
    tm, dc = pre.shape
    j = pl.program_id(1)

    @pl.when(j == 0)
    def _():
        hbuf[0:_CONV_HIST, :] = jnp.zeros((_CONV_HIST, dc), F32)

    @pl.when(j > 0)
    def _():
        hbuf[0:_CONV_HIST, :] = hbuf[tm:tm + _CONV_HIST, :]

    a = p_ref[:, 0:dc]
    gate = p_ref[:, dc:2 * dc]
    hbuf[_CONV_HIST:_CONV_HIST + tm, :] = a * jax.nn.sigmoid(gate)

    first = _CONV_HIST - (width - 1)

    def body(r, carry):
        base = pl.multiple_of(r * _CONV_SUB, _CONV_SUB)
        win = hbuf[pl.ds(base, 2 * _CONV_SUB), :]
        acc = jnp.broadcast_to(cb_ref[...], (_CONV_SUB, dc))
        for ph in range(SUBLANES):
            offs = [first + k for k in range(width) if (first + k) % SUBLANES == ph]
            if not offs:
                continue
            span = max(offs) - ph + _CONV_SUB
            sh = win[ph:ph + span, :]
            for o in offs:
                k = o - first
                acc = acc + sh[o - ph:o - ph + _CONV_SUB, :] * w_ref[k:k + 1, :]
        pre[pl.ds(base, _CONV_SUB), :] = acc
        return carry

    lax.fori_loop(0, tm // _CONV_SUB, body, 0)

    h = pre[...]
    gm = gm_ref[...]
    mu = _dot2(h, gm)
    d = h - mu
    var = _dot2(d * d, gm)
    hn = d * lax.rsqrt(var + EPS) * ng_ref[...] + nb_ref[...]
    o_ref[...] = (hn * jax.nn.sigmoid(hn)).astype(o_ref.dtype)


def _conv(proj3, conv_w, conv_b, norm_g, norm_b, tm=512):
    b, s, _ = proj3.shape
    width, dc = conv_w.shape
    assert width - 1 <= _CONV_HIST and tm % _CONV_SUB == 0 and s % tm == 0
    gsz = dc // CONV_GROUPS
    gid = np.arange(dc) // gsz
    gm = jnp.asarray((gid[:, None] == gid[None, :]).astype(np.float32) / gsz, BF16)
    return pl.pallas_call(
        functools.partial(_conv_kernel, width=width),
        grid=(b, s // tm),
        in_specs=[pl.BlockSpec((None, tm, 2 * dc), lambda i, j: (i, j, 0)),
                  pl.BlockSpec((width, dc), lambda i, j: (0, 0)),
                  pl.BlockSpec((1, dc), lambda i, j: (0, 0)),
                  pl.BlockSpec((1, dc), lambda i, j: (0, 0)),
                  pl.BlockSpec((1, dc), lambda i, j: (0, 0)),
                  pl.BlockSpec((dc, dc), lambda i, j: (0, 0))],
        out_specs=pl.BlockSpec((None, tm, dc), lambda i, j: (i, j, 0)),
        out_shape=jax.ShapeDtypeStruct((b, s, dc), BF16),
        scratch_shapes=[pltpu.VMEM((tm + _CONV_HIST, dc), F32),
                        pltpu.VMEM((tm, dc), F32)],
        compiler_params=_cparams(("arbitrary", "arbitrary")),
        name="conformer_conv",
    )(proj3, conv_w, conv_b, norm_g, norm_b, gm)


def _hgrn_tables(c):
    r = np.arange(c)
    mats = [(r[None, :] <= r[:, None]), (r[None, :] > r[:, None])]
    masks = []
    m = 1
    while m < c:
        blk, pos = r // (2 * m), r % (2 * m)
        anchor = blk * 2 * m + m - 1
        second = pos >= m
        p = np.where(second[:, None],
                     (r[None, :] > anchor[:, None]) & (r[None, :] <= r[:, None]),
                     (r[None, :] > r[:, None]) & (r[None, :] <= anchor[:, None]))
        mats.append(p)
        masks.append((blk[:, None] == blk[None, :]) & second[:, None] & (~second)[None, :])
        m *= 2
    pm = np.concatenate(mats, axis=0).astype(np.float32)
    return jnp.asarray(pm, BF16), jnp.asarray(np.stack(masks).astype(np.float32))


def _hgrn_kernel(qf_ref, ig_ref, lb_ref, ng_ref, pm_ref, mask_ref, o_ref, st_ref, *, heads):
    @pl.when(pl.program_id(1) == 0)
    def _():
        st_ref[...] = jnp.zeros(st_ref.shape, F32)

    for bi in range(o_ref.shape[0]):
        _hgrn_chunk(qf_ref.at[bi], ig_ref.at[bi], lb_ref, ng_ref, pm_ref, mask_ref,
                    o_ref.at[bi], st_ref.at[bi], heads)


def _hgrn_chunk(qf_ref, ig_ref, lb_ref, ng_ref, pm_ref, mask_ref, o_ref, st_ref, heads):
    c, dh = o_ref.shape
    hd = dh // heads
    nlev = mask_ref.shape[0]

    lb = lb_ref[...]
    qin = qf_ref[:, 0:dh]
    z = qf_ref[:, dh:2 * dh]
    vin = ig_ref[:, 0:dh]
    gin = ig_ref[:, dh:2 * dh]

    f = lb + (1.0 - lb) * jax.nn.sigmoid(z)
    logf = jnp.log(f)
    kk = (1.0 - lb) * jax.nn.sigmoid(-z)
    q = qin * jax.nn.sigmoid(qin)

    pm = pm_ref[...]
    hi = logf.astype(BF16)
    r1 = logf - hi.astype(F32)
    mid = r1.astype(BF16)
    lo = (r1 - mid.astype(F32)).astype(BF16)
    ex = (jnp.dot(pm, hi, preferred_element_type=F32)
          + jnp.dot(pm, mid, preferred_element_type=F32)
          + jnp.dot(pm, lo, preferred_element_type=F32))
    b = ex[0:c]
    suf = ex[c:2 * c]

    for h in range(heads):
        sl = slice(h * hd, (h + 1) * hd)
        qh, kh, vh = q[:, sl], kk[:, sl], vin[:, sl]
        vb = vh.astype(BF16)
        st = st_ref[h]
        qb = (qh * jnp.exp(b[:, sl])).astype(BF16)
        o = lax.dot_general(qb, st.astype(BF16), _NT, preferred_element_type=F32)
        scores = jnp.zeros((c, c), F32)
        for lv in range(nlev):
            el = jnp.exp(ex[(2 + lv) * c:(3 + lv) * c, sl])
            s = lax.dot_general((qh * el).astype(BF16), (kh * el).astype(BF16), _NT,
                                preferred_element_type=F32)
            scores = scores + mask_ref[lv] * s
        o = o + jnp.dot(scores.astype(BF16), vb, preferred_element_type=F32)
        o = o + jnp.sum(qh * kh, axis=-1, keepdims=True) * vh
        kd = (kh * jnp.exp(suf[:, sl])).astype(BF16)
        st_ref[h] = (st * jnp.exp(b[c - 1:c, sl])
                     + lax.dot_general(vb, kd, _TN, preferred_element_type=F32))
        o = o * lax.rsqrt(jnp.mean(o * o, axis=-1, keepdims=True) + EPS) * ng_ref[:, sl]
        gh = gin[:, sl]
        o_ref[:, sl] = (o * (gh * jax.nn.sigmoid(gh))).astype(o_ref.dtype)


def _hgrn(proj3, lb, norm_g, heads, c=HGRN_CHUNK, nb=4):
    b, s, n = proj3.shape
    dh = lb.shape[1]
    nblk = n // (2 * dh)
    nb = min(nb, b)
    assert b % nb == 0
    pm, masks = _hgrn_tables(c)
    return pl.pallas_call(
        functools.partial(_hgrn_kernel, heads=heads),
        grid=(b // nb, s // c),
        in_specs=[pl.BlockSpec((nb, c, 2 * dh), lambda i, j: (i, j, nblk - 2)),
                  pl.BlockSpec((nb, c, 2 * dh), lambda i, j: (i, j, nblk - 1)),
                  pl.BlockSpec((1, dh), lambda i, j: (0, 0)),
                  pl.BlockSpec((1, dh), lambda i, j: (0, 0)),
                  pl.BlockSpec(pm.shape, lambda i, j: (0, 0)),
                  pl.BlockSpec(masks.shape, lambda i, j: (0, 0, 0))],
        out_specs=pl.BlockSpec((nb, c, dh), lambda i, j: (i, j, 0)),
        out_shape=jax.ShapeDtypeStruct((b, s, dh), BF16),
        scratch_shapes=[pltpu.VMEM((nb, heads, dh // heads, dh // heads), F32)],
        compiler_params=_cparams(("arbitrary", "arbitrary")),
        name="hgrn2",
    )(proj3, proj3, lb, norm_g, pm, masks)


def _sort_network(n):
    pairs = []

    def merge(lo, cnt, r):
        step = 2 * r
        if step < cnt:
            merge(lo, cnt, step)
            merge(lo + r, cnt, step)
            pairs.extend((i, i + r) for i in range(lo + r, lo + cnt - r, step))
        else:
            pairs.append((lo, lo + r))

    def sort(lo, cnt):
        if cnt > 1:
            sort(lo, cnt // 2)
            sort(lo + cnt // 2, cnt // 2)
            merge(lo, cnt, 1)

    sort(0, n)
    return pairs


def _sort_blocks(vs, ids):
    vs, ids = list(vs), list(ids)
    for i, j in _sort_network(len(vs)):
        first = vs[i] >= vs[j]
        vs[i], vs[j] = jnp.maximum(vs[i], vs[j]), jnp.minimum(vs[i], vs[j])
        ids[i], ids[j] = jnp.where(first, ids[i], ids[j]), jnp.where(first, ids[j], ids[i])
    return vs, ids


def _merge_topk(vs, ids, k):
    vs, ids = list(vs), list(ids)
    out_v, out_i = [], []
    for r in range(k):
        mx = jnp.broadcast_to(jnp.max(vs[0], axis=0, keepdims=True), vs[0].shape)
        pick = jnp.where(vs[0] == mx, ids[0], jnp.int32(2 ** 30))
        pick = jnp.broadcast_to(jnp.min(pick, axis=0, keepdims=True), pick.shape)
        out_v.append(mx)
        out_i.append(pick)
        hit = ids[0] == pick
        for j in range(min(k - r - 1, len(vs) - 1)):
            vs[j] = jnp.where(hit, vs[j + 1], vs[j])
            ids[j] = jnp.where(hit, ids[j + 1], ids[j])
    return out_v, out_i


def _rows_from_blocks(blocks):
    sub = lax.broadcasted_iota(jnp.int32, blocks[0].shape, 0)
    out = []
    for g in range(0, len(blocks), SUBLANES):
        acc = blocks[g]
        for r in range(1, SUBLANES):
            acc = jnp.where(sub == r, blocks[g + r], acc)
        out.append(acc)
    return jnp.concatenate(out, axis=0)


def _mid_kernel(x_ref, ya_ref, yb_ref, woa_ref, wob_ref, g_ref, wq_ref, keys_ref,
                h2_ref, xw_ref, eidx_ref, gate_ref, q_scr, e_scr, g_scr, *, n_keys):
    k = PEER_TOPK
    kq = int(np.sqrt(k))
    tm, d = h2_ref.shape
    heads = keys_ref.shape[0] // 2
    kd = keys_ref.shape[2]
    nblk = n_keys // SUBLANES

    h2 = (x_ref[...]
          + jnp.dot(ya_ref[...], woa_ref[...], preferred_element_type=F32)
          + jnp.dot(yb_ref[...], wob_ref[...], preferred_element_type=F32))
    h2_ref[...] = h2
    xn = h2 * lax.rsqrt(jnp.mean(h2 * h2, axis=-1, keepdims=True) + EPS) * g_ref[...]
    for p in range(2):
        half = jnp.concatenate([xn[:, c * LANES:(c + 1) * LANES] for c in range(p, d // LANES, 2)],
                               axis=1)
        xw_ref[:, p * SUBLANES:(p + 1) * SUBLANES, :] = half.reshape(tm // SUBLANES, SUBLANES, d // 2)
    q_scr[...] = jnp.dot(xn.astype(BF16), wq_ref[...], preferred_element_type=F32)

    sub = lax.broadcasted_iota(jnp.int32, (SUBLANES, tm), 0)
    low = sub < kq

    def head(h, carry):
        tops = []
        for p in range(2):
            col = pl.multiple_of((2 * h + p) * kd, kd)
            qhp = q_scr[:, pl.ds(col, kd)].astype(BF16)
            sc = lax.dot_general(keys_ref[2 * h + p], qhp, _NT,
                                 preferred_element_type=F32)
            vs = [sc[i * SUBLANES:(i + 1) * SUBLANES] for i in range(nblk)]
            ids = [sub + i * SUBLANES for i in range(nblk)]
            tops.append(_merge_topk(*_sort_blocks(vs, ids), k))
        (v1, i1), (v2, i2) = tops
        colv, coli = v1[0], i1[0]
        for s_ in range(1, 2 * kq):
            src_v, src_i = (v1[s_], i1[s_]) if s_ < kq else (v2[s_ - kq], i2[s_ - kq])
            colv = jnp.where(sub == s_, src_v, colv)
            coli = jnp.where(sub == s_, src_i, coli)
        cand, cid = [], []
        for j in range(k):
            if j + kq < k:
                cand.append(colv + jnp.where(low, v2[j], v1[j + kq]))
                cid.append(jnp.where(low, coli * n_keys + i2[j], i1[j + kq] * n_keys + coli))
            else:
                cand.append(jnp.where(low, colv + v2[j], -jnp.inf))
                cid.append(jnp.where(low, coli * n_keys + i2[j], -1))
        tv, te = _merge_topk(cand, cid, k)
        ez = [jnp.exp(v - tv[0]) for v in tv]
        den = ez[0]
        for e_ in ez[1:]:
            den = den + e_
        row0 = pl.multiple_of(h * k, k)
        e_scr[pl.ds(row0, k), :] = _rows_from_blocks(te) * _SLAB
        g_scr[pl.ds(row0, k), :] = _rows_from_blocks([e_ / den for e_ in ez])
        return carry

    lax.fori_loop(0, heads, head, 0)
    eidx_ref[...] = e_scr[...].T
    gate_ref[...] = g_scr[...].T


def _mid(x2, ya, yb, wo_bf, g, wq_bf, keys_bf, tm=256):
    t, d = x2.shape
    dc = ya.shape[1]
    hp, n_keys, kd = keys_bf.shape
    heads = hp // 2
    nq = wq_bf.shape[1]
    picks = heads * PEER_TOPK
    assert d == SUBLANES * LANES and 2 * int(np.sqrt(PEER_TOPK)) == SUBLANES
    return pl.pallas_call(
        functools.partial(_mid_kernel, n_keys=n_keys),
        grid=(t // tm,),
        in_specs=[pl.BlockSpec((tm, d), lambda i: (i, 0)),
                  pl.BlockSpec((tm, dc), lambda i: (i, 0)),
                  pl.BlockSpec((tm, d - dc), lambda i: (i, 0)),
                  pl.BlockSpec((dc, d), lambda i: (0, 0)),
                  pl.BlockSpec((d - dc, d), lambda i: (0, 0)),
                  pl.BlockSpec((1, d), lambda i: (0, 0)),
                  pl.BlockSpec((d, nq), lambda i: (0, 0)),
                  pl.BlockSpec((hp, n_keys, kd), lambda i: (0, 0, 0))],
        out_specs=[pl.BlockSpec((tm, d), lambda i: (i, 0)),
                   pl.BlockSpec((tm // SUBLANES, 2 * SUBLANES, d // 2), lambda i: (i, 0, 0)),
                   pl.BlockSpec((tm, picks), lambda i: (i, 0)),
                   pl.BlockSpec((tm, picks), lambda i: (i, 0))],
        out_shape=[jax.ShapeDtypeStruct((t, d), F32),
                   jax.ShapeDtypeStruct((t // SUBLANES, 2 * SUBLANES, d // 2), F32),
                   jax.ShapeDtypeStruct((t, picks), jnp.int32),
                   jax.ShapeDtypeStruct((t, picks), F32)],
        scratch_shapes=[pltpu.VMEM((tm, nq), F32),
                        pltpu.VMEM((picks, tm), jnp.int32),
                        pltpu.VMEM((picks, tm), F32)],
        compiler_params=_cparams(("arbitrary",)),
        name="mid",
    )(x2, ya, yb, wo_bf[:dc], wo_bf[dc:], g, wq_bf, keys_bf)


def _peer_consts(picks):
    col = np.arange(picks * SUBLANES)
    diag = (col[None, :] % SUBLANES == np.arange(SUBLANES)[:, None]).astype(np.float32)
    grp = (col[:, None] // SUBLANES == np.arange(picks)[None, :]).astype(np.float32)
    row = np.arange(2 * picks)
    par = ((np.arange(2 * SUBLANES)[None, :] >= SUBLANES) == (row[:, None] % 2 == 1)).astype(np.float32)
    pair = (row[:, None] // 2 == np.arange(picks)[None, :]).astype(np.float32)
    return dict(diag=jnp.asarray(diag), grp_t=jnp.asarray(grp.T, BF16),
                par=jnp.asarray(par), pair=jnp.asarray(pair, BF16))


_PEER_UNROLL = 8


def _gather_rows(eidx_ref, tab_ref, gb_ref, t, picks):
    for j in range(picks):
        row = pl.multiple_of(eidx_ref[t, j], _SLAB)
        gb_ref[j * _SLAB:(j + 1) * _SLAB, :] = tab_ref[pl.ds(row, _SLAB), :]


def _token_pipeline(tb, gather, compute):
    u_n = _PEER_UNROLL
    for u in range(u_n):
        gather(u, u)

    def two_steps(i, carry):
        for half in range(2):
            first = (2 * i + half) * u_n
            for u in range(u_n):
                gather(jnp.minimum(first + u_n + u, tb - 1), (1 - half) * u_n + u)
            for u in range(u_n):
                compute(first + u, half * u_n + u)
        return carry

    lax.fori_loop(0, tb // (2 * u_n), two_steps, 0)


def _peer_down_kernel(eidx_ref, xw_ref, gate_ref, tab_ref, par_ref, pair_ref, w_ref, gb_ref,
                      acc_ref):
    tb, picks = gate_ref.shape
    u_n = _PEER_UNROLL
    sub = lax.broadcasted_iota(jnp.int32, (2 * SUBLANES, 1), 0)

    def gather(t, slot):
        _gather_rows(eidx_ref, tab_ref, gb_ref.at[slot], t, picks)

    def compute(t, slot):
        u = slot % u_n
        lhs = jnp.concatenate(
            [pltpu.bitcast(gb_ref[slot, pl.ds(r, picks, stride=_SLAB), :], BF16)
             for r in range(_SLAB)], axis=1)
        xw = xw_ref[lax.div(t, u_n)]
        xb = jnp.where((sub == u) | (sub == SUBLANES + u), xw, 0.0).astype(BF16)
        part = lax.dot_general(lhs, xb, _NT, preferred_element_type=F32)
        if u == 0:
            acc_ref[...] = part
        else:
            acc_ref[...] += part
        if u == u_n - 1:
            sel = acc_ref[...] * par_ref[...]
            wide = jnp.concatenate([sel, jnp.zeros((2 * picks, LANES - 2 * SUBLANES), F32)], axis=1)
            act2 = _dot2(wide.T[0:2 * SUBLANES], pair_ref[...])
            act = act2[0:SUBLANES] + act2[SUBLANES:2 * SUBLANES]
            t0 = pl.multiple_of(t - (u_n - 1), u_n)
            gelu = 0.5 * act * (1.0 + lax.erf(act * np.float32(np.sqrt(0.5))))
            w_ref[pl.ds(t0, u_n), :] = gate_ref[pl.ds(t0, u_n), :] * gelu

    _token_pipeline(tb, gather, compute)


def _peer_up_kernel(eidx_ref, w_in_ref, h2_ref, g_ref, tab_ref, diag_ref, grpt_ref, o_ref,
                    gb_ref, wx_ref, y_ref):
    tb, picks = w_in_ref.shape
    d = o_ref.shape[1]
    wx_ref[...] = _dot2(w_in_ref[...], grpt_ref[...])

    def gather(t, slot):
        _gather_rows(eidx_ref, tab_ref, gb_ref.at[slot], t, picks)

    def compute(t, slot):
        v = pltpu.bitcast(gb_ref[slot], BF16)
        coef = wx_ref[pl.ds(t, 1), :] * diag_ref[...]
        chi, clo = _split2(coef)
        y_ref[pl.ds(pl.multiple_of(t * SUBLANES, SUBLANES), SUBLANES), :] = (
            jnp.dot(chi, v, preferred_element_type=F32)
            + jnp.dot(clo, v, preferred_element_type=F32))

    _token_pipeline(tb, gather, compute)
    ms = jnp.zeros((tb, 1), F32)
    for c in range(SUBLANES):
        cols = slice(c * LANES, (c + 1) * LANES)
        hc = h2_ref[:, cols] + y_ref[pl.ds(c, tb, stride=SUBLANES), :]
        ms = ms + jnp.sum(hc * hc, axis=-1, keepdims=True)
        o_ref[:, cols] = hc
    o_ref[...] = o_ref[...] * lax.rsqrt(ms / d + EPS) * g_ref[...]


def _table_spec(tab):
    return pl.BlockSpec(tab.shape, lambda i: (0, 0), pipeline_mode=pl.Buffered(1))


def _pack_kernel(t_ref, o_ref, z_ref):
    tr = t_ref.shape[0]
    for c in range(SUBLANES):
        z_ref[pl.ds(c, tr, stride=SUBLANES), :] = t_ref[:, c * LANES:(c + 1) * LANES]
    o_ref[...] = pltpu.bitcast(z_ref[...].astype(BF16), jnp.int32)


def _pack_table(tab, tr=256):
    n, d = tab.shape
    assert d == SUBLANES * LANES and n % tr == 0
    return pl.pallas_call(
        _pack_kernel,
        grid=(n // tr,),
        in_specs=[pl.BlockSpec((tr, d), lambda i: (i, 0))],
        out_specs=pl.BlockSpec((tr * _SLAB, LANES), lambda i: (i, 0)),
        out_shape=jax.ShapeDtypeStruct((n * _SLAB, LANES), jnp.int32),
        scratch_shapes=[pltpu.VMEM((tr * SUBLANES, LANES), F32)],
        compiler_params=_cparams(("arbitrary",)),
        name="pack_table",
    )(tab)


def _peer(xw, h2, eidx_t, gate_t, u_tab, v_tab, final_g, tb=128):
    t, d = h2.shape
    picks = eidx_t.shape[1]
    assert d == SUBLANES * LANES and tb % (2 * _PEER_UNROLL) == 0 and _PEER_UNROLL == SUBLANES
    cst = _peer_consts(picks)
    diag, grpt = cst["diag"], cst["grp_t"]
    smem_idx = pl.BlockSpec((tb, picks), lambda i: (i, 0), memory_space=pltpu.SMEM)
    row2 = pl.BlockSpec((tb, picks), lambda i: (i, 0))
    rowd = pl.BlockSpec((tb, d), lambda i: (i, 0))
    const2 = lambda a: pl.BlockSpec(a.shape, lambda i: (0, 0))

    w = pl.pallas_call(
        _peer_down_kernel,
        grid=(t // tb,),
        in_specs=[smem_idx, pl.BlockSpec((tb // SUBLANES,) + xw.shape[1:], lambda i: (i, 0, 0)), row2,
                  _table_spec(u_tab), const2(cst["par"]), const2(cst["pair"])],
        out_specs=row2,
        out_shape=jax.ShapeDtypeStruct((t, picks), F32),
        scratch_shapes=[pltpu.VMEM((2 * _PEER_UNROLL, picks * _SLAB, LANES), jnp.int32),
                        pltpu.VMEM((2 * picks, 2 * SUBLANES), F32)],
        compiler_params=_cparams(("arbitrary",)),
        name="peer_down",
    )(eidx_t, xw, gate_t, u_tab, cst["par"], cst["pair"])

    return pl.pallas_call(
        _peer_up_kernel,
        grid=(t // tb,),
        in_specs=[smem_idx, row2, rowd, const2(final_g), _table_spec(v_tab), const2(diag),
                  const2(grpt)],
        out_specs=rowd,
        out_shape=jax.ShapeDtypeStruct((t, d), F32),
        scratch_shapes=[pltpu.VMEM((2 * _PEER_UNROLL, picks * _SLAB, LANES), jnp.int32),
                        pltpu.VMEM((tb, picks * SUBLANES), F32),
                        pltpu.VMEM((tb * SUBLANES, LANES), F32)],
        compiler_params=_cparams(("arbitrary",)),
        name="peer_up",
    )(eidx_t, w, h2, final_g, v_tab, diag, grpt)


def kernel(x, norm_mix_g, w_in, conv_w, conv_b, conv_norm_g, conv_norm_b, hgrn_lb_logits,
           hgrn_norm_g, w_out, norm_ffn_g, peer_w_query, peer_sub_keys, peer_u, peer_v,
           final_norm_g):
    bsz, s, d = x.shape
    depth = w_in.shape[0]
    dc = conv_w.shape[2]
    dh = hgrn_norm_g.shape[1]
    heads_h = dh // LANES
    ph, _, n_keys, kd = peer_sub_keys.shape[1:]
    t = bsz * s
    row = lambda a: a.reshape(1, -1).astype(F32)

    lb_all = jnp.cumsum(jax.nn.softmax(hgrn_lb_logits.astype(F32), axis=0), axis=0)
    h = x.reshape(t, d)
    for l in range(depth):
        proj = _inproj(h, row(norm_mix_g[l]), w_in[l].astype(BF16))
        proj3 = proj.reshape(bsz, s, -1)
        ya = _conv(proj3, conv_w[l], row(conv_b[l]), row(conv_norm_g[l]), row(conv_norm_b[l]))
        yb = _hgrn(proj3, row(lb_all[l]), row(hgrn_norm_g[l]), heads_h)
        h2, xw, offs, gate = _mid(h, ya.reshape(t, dc), yb.reshape(t, dh), w_out[l].astype(BF16),
                                   row(norm_ffn_g[l]), peer_w_query[l].astype(BF16),
                                   peer_sub_keys[l].reshape(ph * 2, n_keys, kd).astype(BF16))
        assert l == depth - 1, "a deeper stack needs the un-normalised residual between layers"
        h = _peer(xw, h2, offs, gate, _pack_table(peer_u[l]), _pack_table(peer_v[l]),
                  row(final_norm_g))
    return h.reshape(bsz, s, d)
```

```python
import functools

import numpy as np
import jax
import jax.numpy as jnp
from jax import lax
from jax.experimental import pallas as pl
from jax.experimental.pallas import tpu as pltpu

F32 = jnp.float32
BF16 = jnp.bfloat16
EPS = 1e-6

LANES = 128
SUBLANES = 8
_SLAB = SUBLANES // 2
CONV_GROUPS = 8
HGRN_CHUNK = 128
PEER_TOPK = 16
VMEM_LIMIT = 48 * 1024 * 1024

_NT = (((1,), (1,)), ((), ()))
_TN = (((0,), (0,)), ((), ()))


def _cparams(sem):
    return pltpu.CompilerParams(dimension_semantics=sem, vmem_limit_bytes=VMEM_LIMIT)


def _split2(x):
    hi = x.astype(BF16)
    lo = (x - hi.astype(F32)).astype(BF16)
    return hi, lo


def _dot2(x, w):
    hi, lo = _split2(x)
    return (jnp.dot(hi, w, preferred_element_type=F32)
            + jnp.dot(lo, w, preferred_element_type=F32))


def _inproj_kernel(x_ref, g_ref, w_ref, o_ref):
    x = x_ref[...]
    xn = x * lax.rsqrt(jnp.mean(x * x, axis=-1, keepdims=True) + EPS) * g_ref[...]
    o_ref[...] = jnp.dot(xn.astype(BF16), w_ref[...], preferred_element_type=F32)


def _inproj(x2, g, w_bf, tm=256):
    t, d = x2.shape
    n = w_bf.shape[1]
    return pl.pallas_call(
        _inproj_kernel,
        grid=(t // tm,),
        in_specs=[pl.BlockSpec((tm, d), lambda i: (i, 0)),
                  pl.BlockSpec((1, d), lambda i: (0, 0)),
                  pl.BlockSpec((d, n), lambda i: (0, 0))],
        out_specs=pl.BlockSpec((tm, n), lambda i: (i, 0)),
        out_shape=jax.ShapeDtypeStruct((t, n), F32),
        compiler_params=_cparams(("arbitrary",)),
        name="inproj",
    )(x2, g, w_bf)


_CONV_SUB = 32
_CONV_HIST = 32


def _conv_kernel(p_ref, w_ref, cb_ref, ng_ref, nb_ref, gm_ref, o_ref, hbuf, pre, *, width):
    tm, dc = pre.shape
    j = pl.program_id(1)

    @pl.when(j == 0)
    def _():
        hbuf[0:_CONV_HIST, :] = jnp.zeros((_CONV_HIST, dc), F32)

    @pl.when(j > 0)
    def _():
        hbuf[0:_CONV_HIST, :] = hbuf[tm:tm + _CONV_HIST, :]

    a = p_ref[:, 0:dc]
    gate = p_ref[:, dc:2 * dc]
    hbuf[_CONV_HIST:_CONV_HIST + tm, :] = a * jax.nn.sigmoid(gate)

    first = _CONV_HIST - (width - 1)

    def body(r, carry):
        base = pl.multiple_of(r * _CONV_SUB, _CONV_SUB)
        win = hbuf[pl.ds(base, 2 * _CONV_SUB), :]
        acc = jnp.broadcast_to(cb_ref[...], (_CONV_SUB, dc))
        for ph in range(SUBLANES):
            offs = [first + k for k in range(width) if (first + k) % SUBLANES == ph]
            if not offs:
                continue
            span = max(offs) - ph + _CONV_SUB
            sh = win[ph:ph + span, :]
            for o in offs:
                k = o - first
                acc = acc + sh[o - ph:o - ph + _CONV_SUB, :] * w_ref[k:k + 1, :]
        pre[pl.ds(base, _CONV_SUB), :] = acc
        return carry

    lax.fori_loop(0, tm // _CONV_SUB, body, 0)

    h = pre[...]
    gm = gm_ref[...]
    mu = _dot2(h, gm)
    d = h - mu
    var = _dot2(d * d, gm)
    hn = d * lax.rsqrt(var + EPS) * ng_ref[...] + nb_ref[...]
    o_ref[...] = (hn * jax.nn.sigmoid(hn)).astype(o_ref.dtype)


def _conv(proj3, conv_w, conv_b, norm_g, norm_b, tm=512):
    b, s, _ = proj3.shape
    width, dc = conv_w.shape
    assert width - 1 <= _CONV_HIST and tm % _CONV_SUB == 0 and s % tm == 0
    gsz = dc // CONV_GROUPS
    gid = np.arange(dc) // gsz
    gm = jnp.asarray((gid[:, None] == gid[None, :]).astype(np.float32) / gsz, BF16)
    return pl.pallas_call(
        functools.partial(_conv_kernel, width=width),
        grid=(b, s // tm),
        in_specs=[pl.BlockSpec((None, tm, 2 * dc), lambda i, j: (i, j, 0)),
                  pl.BlockSpec((width, dc), lambda i, j: (0, 0)),
                  pl.BlockSpec((1, dc), lambda i, j: (0, 0)),
                  pl.BlockSpec((1, dc), lambda i, j: (0, 0)),
                  pl.BlockSpec((1, dc), lambda i, j: (0, 0)),
                  pl.BlockSpec((dc, dc), lambda i, j: (0, 0))],
        out_specs=pl.BlockSpec((None, tm, dc), lambda i, j: (i, j, 0)),
        out_shape=jax.ShapeDtypeStruct((b, s, dc), BF16),
        scratch_shapes=[pltpu.VMEM((tm + _CONV_HIST, dc), F32),
                        pltpu.VMEM((tm, dc), F32)],
        compiler_params=_cparams(("arbitrary", "arbitrary")),
        name="conformer_conv",
    )(proj3, conv_w, conv_b, norm_g, norm_b, gm)


def _hgrn_tables(c):
    r = np.arange(c)
    mats = [(r[None, :] <= r[:, None]), (r[None, :] > r[:, None])]
    masks = []
    m = 1
    while m < c:
        blk, pos = r // (2 * m), r % (2 * m)
        anchor = blk * 2 * m + m - 1
        second = pos >= m
        p = np.where(second[:, None],
                     (r[None, :] > anchor[:, None]) & (r[None, :] <= r[:, None]),
                     (r[None, :] > r[:, None]) & (r[None, :] <= anchor[:, None]))
        mats.append(p)
        masks.append((blk[:, None] == blk[None, :]) & second[:, None] & (~second)[None, :])
        m *= 2
    pm = np.concatenate(mats, axis=0).astype(np.float32)
    return jnp.asarray(pm, BF16), jnp.asarray(np.stack(masks).astype(np.float32))


def _hgrn_kernel(qf_ref, ig_ref, lb_ref, ng_ref, pm_ref, mask_ref, o_ref, st_ref, *, heads):
    @pl.when(pl.program_id(1) == 0)
    def _():
        st_ref[...] = jnp.zeros(st_ref.shape, F32)

    for bi in range(o_ref.shape[0]):
        _hgrn_chunk(qf_ref.at[bi], ig_ref.at[bi], lb_ref, ng_ref, pm_ref, mask_ref,
                    o_ref.at[bi], st_ref.at[bi], heads)


def _hgrn_chunk(qf_ref, ig_ref, lb_ref, ng_ref, pm_ref, mask_ref, o_ref, st_ref, heads):
    c, dh = o_ref.shape
    hd = dh // heads
    nlev = mask_ref.shape[0]

    lb = lb_ref[...]
    qin = qf_ref[:, 0:dh]
    z = qf_ref[:, dh:2 * dh]
    vin = ig_ref[:, 0:dh]
    gin = ig_ref[:, dh:2 * dh]

    f = lb + (1.0 - lb) * jax.nn.sigmoid(z)
    logf = jnp.log(f)
    kk = (1.0 - lb) * jax.nn.sigmoid(-z)
    q = qin * jax.nn.sigmoid(qin)

    pm = pm_ref[...]
    hi = logf.astype(BF16)
    r1 = logf - hi.astype(F32)
    mid = r1.astype(BF16)
    lo = (r1 - mid.astype(F32)).astype(BF16)
    ex = (jnp.dot(pm, hi, preferred_element_type=F32)
          + jnp.dot(pm, mid, preferred_element_type=F32)
          + jnp.dot(pm, lo, preferred_element_type=F32))
    b = ex[0:c]
    suf = ex[c:2 * c]

    for h in range(heads):
        sl = slice(h * hd, (h + 1) * hd)
        qh, kh, vh = q[:, sl], kk[:, sl], vin[:, sl]
        vb = vh.astype(BF16)
        st = st_ref[h]
        qb = (qh * jnp.exp(b[:, sl])).astype(BF16)
        o = lax.dot_general(qb, st.astype(BF16), _NT, preferred_element_type=F32)
        scores = jnp.zeros((c, c), F32)
        for lv in range(nlev):
            el = jnp.exp(ex[(2 + lv) * c:(3 + lv) * c, sl])
            s = lax.dot_general((qh * el).astype(BF16), (kh * el).astype(BF16), _NT,
                                preferred_element_type=F32)
            scores = scores + mask_ref[lv] * s
        o = o + jnp.dot(scores.astype(BF16), vb, preferred_element_type=F32)
        o = o + jnp.sum(qh * kh, axis=-1, keepdims=True) * vh
        kd = (kh * jnp.exp(suf[:, sl])).astype(BF16)
        st_ref[h] = (st * jnp.exp(b[c - 1:c, sl])
                     + lax.dot_general(vb, kd, _TN, preferred_element_type=F32))
        o = o * lax.rsqrt(jnp.mean(o * o, axis=-1, keepdims=True) + EPS) * ng_ref[:, sl]
        gh = gin[:, sl]
        o_ref[:, sl] = (o * (gh * jax.nn.sigmoid(gh))).astype(o_ref.dtype)


def _hgrn(proj3, lb, norm_g, heads, c=HGRN_CHUNK, nb=4):
    b, s, n = proj3.shape
    dh = lb.shape[1]
    nblk = n // (2 * dh)
    nb = min(nb, b)
    assert b % nb == 0
    pm, masks = _hgrn_tables(c)
    return pl.pallas_call(
        functools.partial(_hgrn_kernel, heads=heads),
        grid=(b // nb, s // c),
        in_specs=[pl.BlockSpec((nb, c, 2 * dh), lambda i, j: (i, j, nblk - 2)),
                  pl.BlockSpec((nb, c, 2 * dh), lambda i, j: (i, j, nblk - 1)),
                  pl.BlockSpec((1, dh), lambda i, j: (0, 0)),
                  pl.BlockSpec((1, dh), lambda i, j: (0, 0)),
                  pl.BlockSpec(pm.shape, lambda i, j: (0, 0)),
                  pl.BlockSpec(masks.shape, lambda i, j: (0, 0, 0))],
        out_specs=pl.BlockSpec((nb, c, dh), lambda i, j: (i, j, 0)),
        out_shape=jax.ShapeDtypeStruct((b, s, dh), BF16),
        scratch_shapes=[pltpu.VMEM((nb, heads, dh // heads, dh // heads), F32)],
        compiler_params=_cparams(("arbitrary", "arbitrary")),
        name="hgrn2",
    )(proj3, proj3, lb, norm_g, pm, masks)


def _sort_network(n):
    pairs = []

    def merge(lo, cnt, r):
        step = 2 * r
        if step < cnt:
            merge(lo, cnt, step)
            merge(lo + r, cnt, step)
            pairs.extend((i, i + r) for i in range(lo + r, lo + cnt - r, step))
        else:
            pairs.append((lo, lo + r))

    def sort(lo, cnt):
        if cnt > 1:
            sort(lo, cnt // 2)
            sort(lo + cnt // 2, cnt // 2)
            merge(lo, cnt, 1)

    sort(0, n)
    return pairs


def _sort_blocks(vs, ids):
    vs, ids = list(vs), list(ids)
    for i, j in _sort_network(len(vs)):
        first = vs[i] >= vs[j]
        vs[i], vs[j] = jnp.maximum(vs[i], vs[j]), jnp.minimum(vs[i], vs[j])
        ids[i], ids[j] = jnp.where(first, ids[i], ids[j]), jnp.where(first, ids[j], ids[i])
    return vs, ids


def _merge_topk(vs, ids, k):
    vs, ids = list(vs), list(ids)
    out_v, out_i = [], []
    for r in range(k):
        mx = jnp.broadcast_to(jnp.max(vs[0], axis=0, keepdims=True), vs[0].shape)
        pick = jnp.where(vs[0] == mx, ids[0], jnp.int32(2 ** 30))
        pick = jnp.broadcast_to(jnp.min(pick, axis=0, keepdims=True), pick.shape)
        out_v.append(mx)
        out_i.append(pick)
        hit = ids[0] == pick
        for j in range(min(k - r - 1, len(vs) - 1)):
            vs[j] = jnp.where(hit, vs[j + 1], vs[j])
            ids[j] = jnp.where(hit, ids[j + 1], ids[j])
    return out_v, out_i


def _rows_from_blocks(blocks):
    sub = lax.broadcasted_iota(jnp.int32, blocks[0].shape, 0)
    out = []
    for g in range(0, len(blocks), SUBLANES):
        acc = blocks[g]
        for r in range(1, SUBLANES):
            acc = jnp.where(sub == r, blocks[g + r], acc)
        out.append(acc)
    return jnp.concatenate(out, axis=0)


def _mid_kernel(x_ref, ya_ref, yb_ref, woa_ref, wob_ref, g_ref, wq_ref, keys_ref,
                h2_ref, xn8_ref, eidx_ref, gate_ref, q_scr, e_scr, g_scr, *, n_keys):
    k = PEER_TOPK
    kq = int(np.sqrt(k))
    tm, d = h2_ref.shape
    heads = keys_ref.shape[0] // 2
    kd = keys_ref.shape[2]
    nblk = n_keys // SUBLANES

    h2 = (x_ref[...]
          + jnp.dot(ya_ref[...], woa_ref[...], preferred_element_type=F32)
          + jnp.dot(yb_ref[...], wob_ref[...], preferred_element_type=F32))
    h2_ref[...] = h2
    xn = h2 * lax.rsqrt(jnp.mean(h2 * h2, axis=-1, keepdims=True) + EPS) * g_ref[...]
    for c in range(d // LANES):
        xn8_ref[pl.ds(c, tm, stride=d // LANES), :] = xn[:, c * LANES:(c + 1) * LANES]
    q_scr[...] = jnp.dot(xn.astype(BF16), wq_ref[...], preferred_element_type=F32)

    sub = lax.broadcasted_iota(jnp.int32, (SUBLANES, tm), 0)
    low = sub < kq

    def head(h, carry):
        tops = []
        for p in range(2):
            col = pl.multiple_of((2 * h + p) * kd, kd)
            qhp = q_scr[:, pl.ds(col, kd)].astype(BF16)
            sc = lax.dot_general(keys_ref[2 * h + p], qhp, _NT,
                                 preferred_element_type=F32)
            vs = [sc[i * SUBLANES:(i + 1) * SUBLANES] for i in range(nblk)]
            ids = [sub + i * SUBLANES for i in range(nblk)]
            tops.append(_merge_topk(*_sort_blocks(vs, ids), k))
        (v1, i1), (v2, i2) = tops
        colv, coli = v1[0], i1[0]
        for s_ in range(1, 2 * kq):
            src_v, src_i = (v1[s_], i1[s_]) if s_ < kq else (v2[s_ - kq], i2[s_ - kq])
            colv = jnp.where(sub == s_, src_v, colv)
            coli = jnp.where(sub == s_, src_i, coli)
        cand, cid = [], []
        for j in range(k):
            if j + kq < k:
                cand.append(colv + jnp.where(low, v2[j], v1[j + kq]))
                cid.append(jnp.where(low, coli * n_keys + i2[j], i1[j + kq] * n_keys + coli))
            else:
                cand.append(jnp.where(low, colv + v2[j], -jnp.inf))
                cid.append(jnp.where(low, coli * n_keys + i2[j], -1))
        tv, te = _merge_topk(cand, cid, k)
        ez = [jnp.exp(v - tv[0]) for v in tv]
        den = ez[0]
        for e_ in ez[1:]:
            den = den + e_
        row0 = pl.multiple_of(h * k, k)
        e_scr[pl.ds(row0, k), :] = _rows_from_blocks(te) * _SLAB
        g_scr[pl.ds(row0, k), :] = _rows_from_blocks([e_ / den for e_ in ez])
        return carry

    lax.fori_loop(0, heads, head, 0)
    eidx_ref[...] = e_scr[...].T
    gate_ref[...] = g_scr[...].T


def _mid(x2, ya, yb, wo_bf, g, wq_bf, keys_bf, tm=256):
    t, d = x2.shape
    dc = ya.shape[1]
    hp, n_keys, kd = keys_bf.shape
    heads = hp // 2
    nq = wq_bf.shape[1]
    picks = heads * PEER_TOPK
    assert d == SUBLANES * LANES and 2 * int(np.sqrt(PEER_TOPK)) == SUBLANES
    return pl.pallas_call(
        functools.partial(_mid_kernel, n_keys=n_keys),
        grid=(t // tm,),
        in_specs=[pl.BlockSpec((tm, d), lambda i: (i, 0)),
                  pl.BlockSpec((tm, dc), lambda i: (i, 0)),
                  pl.BlockSpec((tm, d - dc), lambda i: (i, 0)),
                  pl.BlockSpec((dc, d), lambda i: (0, 0)),
                  pl.BlockSpec((d - dc, d), lambda i: (0, 0)),
                  pl.BlockSpec((1, d), lambda i: (0, 0)),
                  pl.BlockSpec((d, nq), lambda i: (0, 0)),
                  pl.BlockSpec((hp, n_keys, kd), lambda i: (0, 0, 0))],
        out_specs=[pl.BlockSpec((tm, d), lambda i: (i, 0)),
                   pl.BlockSpec((tm * SUBLANES, LANES), lambda i: (i, 0)),
                   pl.BlockSpec((tm, picks), lambda i: (i, 0)),
                   pl.BlockSpec((tm, picks), lambda i: (i, 0))],
        out_shape=[jax.ShapeDtypeStruct((t, d), F32),
                   jax.ShapeDtypeStruct((t * SUBLANES, LANES), F32),
                   jax.ShapeDtypeStruct((t, picks), jnp.int32),
                   jax.ShapeDtypeStruct((t, picks), F32)],
        scratch_shapes=[pltpu.VMEM((tm, nq), F32),
                        pltpu.VMEM((picks, tm), jnp.int32),
                        pltpu.VMEM((picks, tm), F32)],
        compiler_params=_cparams(("arbitrary",)),
        name="mid",
    )(x2, ya, yb, wo_bf[:dc], wo_bf[dc:], g, wq_bf, keys_bf)


def _peer_consts(picks):
    col = np.arange(picks * SUBLANES)
    diag = (col[None, :] % SUBLANES == np.arange(SUBLANES)[:, None]).astype(np.float32)
    grp = (col[:, None] // SUBLANES == np.arange(picks)[None, :]).astype(np.float32)
    return jnp.asarray(diag), jnp.asarray(grp, BF16), jnp.asarray(grp.T, BF16)


_PEER_UNROLL = 8


def _gather_rows(eidx_ref, tab_ref, gb_ref, t, picks):
    for j in range(picks):
        row = pl.multiple_of(eidx_ref[t, j], _SLAB)
        gb_ref[j * _SLAB:(j + 1) * _SLAB, :] = tab_ref[pl.ds(row, _SLAB), :]


def _token_pipeline(tb, gather, compute):
    u_n = _PEER_UNROLL
    for u in range(u_n):
        gather(u, u)

    def two_steps(i, carry):
        for half in range(2):
            first = (2 * i + half) * u_n
            for u in range(u_n):
                gather(jnp.minimum(first + u_n + u, tb - 1), (1 - half) * u_n + u)
            for u in range(u_n):
                compute(first + u, half * u_n + u)
        return carry

    lax.fori_loop(0, tb // (2 * u_n), two_steps, 0)


def _peer_down_kernel(eidx_ref, x_ref, gate_ref, tab_ref, diag_ref, grp_ref, w_ref, gb_ref, z_ref):
    tb, picks = gate_ref.shape

    def gather(t, slot):
        _gather_rows(eidx_ref, tab_ref, gb_ref.at[slot], t, picks)

    def compute(t, slot):
        w = pltpu.bitcast(gb_ref[slot], BF16)
        xhi, xlo = _split2(x_ref[t])
        part = (lax.dot_general(xhi, w, _NT, preferred_element_type=F32)
                + lax.dot_general(xlo, w, _NT, preferred_element_type=F32))
        z_ref[t] = part * diag_ref[...]

    _token_pipeline(tb, gather, compute)
    z = z_ref[...].reshape(tb * SUBLANES, picks * SUBLANES)
    act = jnp.sum(_dot2(z, grp_ref[...]).reshape(tb, SUBLANES, picks), axis=1)
    gelu = 0.5 * act * (1.0 + lax.erf(act * np.float32(np.sqrt(0.5))))
    w_ref[...] = gate_ref[...] * gelu


def _peer_up_kernel(eidx_ref, w_in_ref, h2_ref, g_ref, tab_ref, diag_ref, grpt_ref, o_ref,
                    gb_ref, wx_ref, y_ref):
    tb, picks = w_in_ref.shape
    d = o_ref.shape[1]
    wx_ref[...] = _dot2(w_in_ref[...], grpt_ref[...])

    def gather(t, slot):
        _gather_rows(eidx_ref, tab_ref, gb_ref.at[slot], t, picks)

    def compute(t, slot):
        v = pltpu.bitcast(gb_ref[slot], BF16)
        coef = wx_ref[pl.ds(t, 1), :] * diag_ref[...]
        chi, clo = _split2(coef)
        y_ref[pl.ds(pl.multiple_of(t * SUBLANES, SUBLANES), SUBLANES), :] = (
            jnp.dot(chi, v, preferred_element_type=F32)
            + jnp.dot(clo, v, preferred_element_type=F32))

    _token_pipeline(tb, gather, compute)
    ms = jnp.zeros((tb, 1), F32)
    for c in range(SUBLANES):
        cols = slice(c * LANES, (c + 1) * LANES)
        hc = h2_ref[:, cols] + y_ref[pl.ds(c, tb, stride=SUBLANES), :]
        ms = ms + jnp.sum(hc * hc, axis=-1, keepdims=True)
        o_ref[:, cols] = hc
    o_ref[...] = o_ref[...] * lax.rsqrt(ms / d + EPS) * g_ref[...]


def _table_spec(tab):
    return pl.BlockSpec(tab.shape, lambda i: (0, 0), pipeline_mode=pl.Buffered(1))


def _pack_kernel(t_ref, o_ref, z_ref):
    tr = t_ref.shape[0]
    for c in range(SUBLANES):
        z_ref[pl.ds(c, tr, stride=SUBLANES), :] = t_ref[:, c * LANES:(c + 1) * LANES]
    o_ref[...] = pltpu.bitcast(z_ref[...].astype(BF16), jnp.int32)


def _pack_table(tab, tr=256):
    n, d = tab.shape
    assert d == SUBLANES * LANES and n % tr == 0
    return pl.pallas_call(
        _pack_kernel,
        grid=(n // tr,),
        in_specs=[pl.BlockSpec((tr, d), lambda i: (i, 0))],
        out_specs=pl.BlockSpec((tr * _SLAB, LANES), lambda i: (i, 0)),
        out_shape=jax.ShapeDtypeStruct((n * _SLAB, LANES), jnp.int32),
        scratch_shapes=[pltpu.VMEM((tr * SUBLANES, LANES), F32)],
        compiler_params=_cparams(("arbitrary",)),
        name="pack_table",
    )(tab)


def _peer(xn8, h2, eidx_t, gate_t, u_tab, v_tab, final_g, tb=128):
    t, d = h2.shape
    picks = eidx_t.shape[1]
    assert d == SUBLANES * LANES and tb % (2 * _PEER_UNROLL) == 0
    diag, grp, grpt = _peer_consts(picks)
    x8 = xn8.reshape(t, SUBLANES, LANES)
    smem_idx = pl.BlockSpec((tb, picks), lambda i: (i, 0), memory_space=pltpu.SMEM)
    row3 = pl.BlockSpec((tb, SUBLANES, LANES), lambda i: (i, 0, 0))
    row2 = pl.BlockSpec((tb, picks), lambda i: (i, 0))
    rowd = pl.BlockSpec((tb, d), lambda i: (i, 0))
    const2 = lambda a: pl.BlockSpec(a.shape, lambda i: (0, 0))

    w = pl.pallas_call(
        _peer_down_kernel,
        grid=(t // tb,),
        in_specs=[smem_idx, row3, row2, _table_spec(u_tab), const2(diag), const2(grp)],
        out_specs=row2,
        out_shape=jax.ShapeDtypeStruct((t, picks), F32),
        scratch_shapes=[pltpu.VMEM((2 * _PEER_UNROLL, picks * _SLAB, LANES), jnp.int32),
                        pltpu.VMEM((tb, SUBLANES, picks * SUBLANES), F32)],
        compiler_params=_cparams(("arbitrary",)),
        name="peer_down",
    )(eidx_t, x8, gate_t, u_tab, diag, grp)

    return pl.pallas_call(
        _peer_up_kernel,
        grid=(t // tb,),
        in_specs=[smem_idx, row2, rowd, const2(final_g), _table_spec(v_tab), const2(diag),
                  const2(grpt)],
        out_specs=rowd,
        out_shape=jax.ShapeDtypeStruct((t, d), F32),
        scratch_shapes=[pltpu.VMEM((2 * _PEER_UNROLL, picks * _SLAB, LANES), jnp.int32),
                        pltpu.VMEM((tb, picks * SUBLANES), F32),
                        pltpu.VMEM((tb * SUBLANES, LANES), F32)],
        compiler_params=_cparams(("arbitrary",)),
        name="peer_up",
    )(eidx_t, w, h2, final_g, v_tab, diag, grpt)


def kernel(x, norm_mix_g, w_in, conv_w, conv_b, conv_norm_g, conv_norm_b, hgrn_lb_logits,
           hgrn_norm_g, w_out, norm_ffn_g, peer_w_query, peer_sub_keys, peer_u, peer_v,
           final_norm_g):
    bsz, s, d = x.shape
    depth = w_in.shape[0]
    dc = conv_w.shape[2]
    dh = hgrn_norm_g.shape[1]
    heads_h = dh // LANES
    ph, _, n_keys, kd = peer_sub_keys.shape[1:]
    t = bsz * s
    row = lambda a: a.reshape(1, -1).astype(F32)

    lb_all = jnp.cumsum(jax.nn.softmax(hgrn_lb_logits.astype(F32), axis=0), axis=0)
    h = x.reshape(t, d)
    for l in range(depth):
        proj = _inproj(h, row(norm_mix_g[l]), w_in[l].astype(BF16))
        proj3 = proj.reshape(bsz, s, -1)
        ya = _conv(proj3, conv_w[l], row(conv_b[l]), row(conv_norm_g[l]), row(conv_norm_b[l]))
        yb = _hgrn(proj3, row(lb_all[l]), row(hgrn_norm_g[l]), heads_h)
        h2, xn8, offs, gate = _mid(h, ya.reshape(t, dc), yb.reshape(t, dh), w_out[l].astype(BF16),
                                   row(norm_ffn_g[l]), peer_w_query[l].astype(BF16),
                                   peer_sub_keys[l].reshape(ph * 2, n_keys, kd).astype(BF16))
        assert l == depth - 1, "a deeper stack needs the un-normalised residual between layers"
        h = _peer(xn8, h2, offs, gate, _pack_table(peer_u[l]), _pack_table(peer_v[l]),
                  row(final_norm_g))
    return h.reshape(bsz, s, d)
```

```python
import functools

import numpy as np
import jax
import jax.numpy as jnp
from jax import lax
from jax.experimental import pallas as pl
from jax.experimental.pallas import tpu as pltpu

F32 = jnp.float32
BF16 = jnp.bfloat16
EPS = 1e-6

LANES = 128
SUBLANES = 8
_SLAB = SUBLANES // 2
CONV_GROUPS = 8
HGRN_CHUNK = 128
PEER_TOPK = 16
VMEM_LIMIT = 48 * 1024 * 1024

_NT = (((1,), (1,)), ((), ()))
_TN = (((0,), (0,)), ((), ()))


def _cparams(sem):
    return pltpu.CompilerParams(dimension_semantics=sem, vmem_limit_bytes=VMEM_LIMIT)


def _split2(x):
    hi = x.astype(BF16)
    lo = (x - hi.astype(F32)).astype(BF16)
    return hi, lo


def _dot2(x, w):
    hi, lo = _split2(x)
    return (jnp.dot(hi, w, preferred_element_type=F32)
            + jnp.dot(lo, w, preferred_element_type=F32))


def _inproj_kernel(x_ref, g_ref, w_ref, o_ref):
    x = x_ref[...]
    xn = x * lax.rsqrt(jnp.mean(x * x, axis=-1, keepdims=True) + EPS) * g_ref[...]
    o_ref[...] = jnp.dot(xn.astype(BF16), w_ref[...], preferred_element_type=F32)


def _inproj(x2, g, w_bf, tm=256):
    t, d = x2.shape
    n = w_bf.shape[1]
    return pl.pallas_call(
        _inproj_kernel,
        grid=(t // tm,),
        in_specs=[pl.BlockSpec((tm, d), lambda i: (i, 0)),
                  pl.BlockSpec((1, d), lambda i: (0, 0)),
                  pl.BlockSpec((d, n), lambda i: (0, 0))],
        out_specs=pl.BlockSpec((tm, n), lambda i: (i, 0)),
        out_shape=jax.ShapeDtypeStruct((t, n), F32),
        compiler_params=_cparams(("arbitrary",)),
        name="inproj",
    )(x2, g, w_bf)


_CONV_SUB = 32
_CONV_HIST = 32


def _conv_kernel(p_ref, w_ref, cb_ref, ng_ref, nb_ref, gm_ref, o_ref, hbuf, pre, *, width):
    tm, dc = pre.shape
    j = pl.program_id(1)

    @pl.when(j == 0)
    def _():
        hbuf[0:_CONV_HIST, :] = jnp.zeros((_CONV_HIST, dc), F32)

    @pl.when(j > 0)
    def _():
        hbuf[0:_CONV_HIST, :] = hbuf[tm:tm + _CONV_HIST, :]

    a = p_ref[:, 0:dc]
    gate = p_ref[:, dc:2 * dc]
    hbuf[_CONV_HIST:_CONV_HIST + tm, :] = a * jax.nn.sigmoid(gate)

    first = _CONV_HIST - (width - 1)

    def body(r, carry):
        base = pl.multiple_of(r * _CONV_SUB, _CONV_SUB)
        win = hbuf[pl.ds(base, 2 * _CONV_SUB), :]
        acc = jnp.broadcast_to(cb_ref[...], (_CONV_SUB, dc))
        for ph in range(SUBLANES):
            offs = [first + k for k in range(width) if (first + k) % SUBLANES == ph]
            if not offs:
                continue
            span = max(offs) - ph + _CONV_SUB
            sh = win[ph:ph + span, :]
            for o in offs:
                k = o - first
                acc = acc + sh[o - ph:o - ph + _CONV_SUB, :] * w_ref[k:k + 1, :]
        pre[pl.ds(base, _CONV_SUB), :] = acc
        return carry

    lax.fori_loop(0, tm // _CONV_SUB, body, 0)

    h = pre[...]
    gm = gm_ref[...]
    mu = _dot2(h, gm)
    d = h - mu
    var = _dot2(d * d, gm)
    hn = d * lax.rsqrt(var + EPS) * ng_ref[...] + nb_ref[...]
    o_ref[...] = (hn * jax.nn.sigmoid(hn)).astype(o_ref.dtype)


def _conv(proj3, conv_w, conv_b, norm_g, norm_b, tm=512):
    b, s, _ = proj3.shape
    width, dc = conv_w.shape
    assert width - 1 <= _CONV_HIST and tm % _CONV_SUB == 0 and s % tm == 0
    gsz = dc // CONV_GROUPS
    gid = np.arange(dc) // gsz
    gm = jnp.asarray((gid[:, None] == gid[None, :]).astype(np.float32) / gsz, BF16)
    return pl.pallas_call(
        functools.partial(_conv_kernel, width=width),
        grid=(b, s // tm),
        in_specs=[pl.BlockSpec((None, tm, 2 * dc), lambda i, j: (i, j, 0)),
                  pl.BlockSpec((width, dc), lambda i, j: (0, 0)),
                  pl.BlockSpec((1, dc), lambda i, j: (0, 0)),
                  pl.BlockSpec((1, dc), lambda i, j: (0, 0)),
                  pl.BlockSpec((1, dc), lambda i, j: (0, 0)),
                  pl.BlockSpec((dc, dc), lambda i, j: (0, 0))],
        out_specs=pl.BlockSpec((None, tm, dc), lambda i, j: (i, j, 0)),
        out_shape=jax.ShapeDtypeStruct((b, s, dc), BF16),
        scratch_shapes=[pltpu.VMEM((tm + _CONV_HIST, dc), F32),
                        pltpu.VMEM((tm, dc), F32)],
        compiler_params=_cparams(("arbitrary", "arbitrary")),
        name="conformer_conv",
    )(proj3, conv_w, conv_b, norm_g, norm_b, gm)


def _hgrn_tables(c):
    r = np.arange(c)
    mats = [(r[None, :] <= r[:, None]), (r[None, :] > r[:, None])]
    masks = []
    m = 1
    while m < c:
        blk, pos = r // (2 * m), r % (2 * m)
        anchor = blk * 2 * m + m - 1
        second = pos >= m
        p = np.where(second[:, None],
                     (r[None, :] > anchor[:, None]) & (r[None, :] <= r[:, None]),
                     (r[None, :] > r[:, None]) & (r[None, :] <= anchor[:, None]))
        mats.append(p)
        masks.append((blk[:, None] == blk[None, :]) & second[:, None] & (~second)[None, :])
        m *= 2
    pm = np.concatenate(mats, axis=0).astype(np.float32)
    return jnp.asarray(pm, BF16), jnp.asarray(np.stack(masks).astype(np.float32))


def _hgrn_kernel(qf_ref, ig_ref, lb_ref, ng_ref, pm_ref, mask_ref, o_ref, st_ref, *, heads):
    @pl.when(pl.program_id(1) == 0)
    def _():
        st_ref[...] = jnp.zeros(st_ref.shape, F32)

    for bi in range(o_ref.shape[0]):
        _hgrn_chunk(qf_ref.at[bi], ig_ref.at[bi], lb_ref, ng_ref, pm_ref, mask_ref,
                    o_ref.at[bi], st_ref.at[bi], heads)


def _hgrn_chunk(qf_ref, ig_ref, lb_ref, ng_ref, pm_ref, mask_ref, o_ref, st_ref, heads):
    c, dh = o_ref.shape
    hd = dh // heads
    nlev = mask_ref.shape[0]

    lb = lb_ref[...]
    qin = qf_ref[:, 0:dh]
    z = qf_ref[:, dh:2 * dh]
    vin = ig_ref[:, 0:dh]
    gin = ig_ref[:, dh:2 * dh]

    f = lb + (1.0 - lb) * jax.nn.sigmoid(z)
    logf = jnp.log(f)
    kk = (1.0 - lb) * jax.nn.sigmoid(-z)
    q = qin * jax.nn.sigmoid(qin)

    pm = pm_ref[...]
    hi = logf.astype(BF16)
    r1 = logf - hi.astype(F32)
    mid = r1.astype(BF16)
    lo = (r1 - mid.astype(F32)).astype(BF16)
    ex = (jnp.dot(pm, hi, preferred_element_type=F32)
          + jnp.dot(pm, mid, preferred_element_type=F32)
          + jnp.dot(pm, lo, preferred_element_type=F32))
    b = ex[0:c]
    suf = ex[c:2 * c]

    for h in range(heads):
        sl = slice(h * hd, (h + 1) * hd)
        qh, kh, vh = q[:, sl], kk[:, sl], vin[:, sl]
        vb = vh.astype(BF16)
        st = st_ref[h]
        qb = (qh * jnp.exp(b[:, sl])).astype(BF16)
        o = lax.dot_general(qb, st.astype(BF16), _NT, preferred_element_type=F32)
        scores = jnp.zeros((c, c), F32)
        for lv in range(nlev):
            el = jnp.exp(ex[(2 + lv) * c:(3 + lv) * c, sl])
            s = lax.dot_general((qh * el).astype(BF16), (kh * el).astype(BF16), _NT,
                                preferred_element_type=F32)
            scores = scores + mask_ref[lv] * s
        o = o + jnp.dot(scores.astype(BF16), vb, preferred_element_type=F32)
        o = o + jnp.sum(qh * kh, axis=-1, keepdims=True) * vh
        kd = (kh * jnp.exp(suf[:, sl])).astype(BF16)
        st_ref[h] = (st * jnp.exp(b[c - 1:c, sl])
                     + lax.dot_general(vb, kd, _TN, preferred_element_type=F32))
        o = o * lax.rsqrt(jnp.mean(o * o, axis=-1, keepdims=True) + EPS) * ng_ref[:, sl]
        gh = gin[:, sl]
        o_ref[:, sl] = (o * (gh * jax.nn.sigmoid(gh))).astype(o_ref.dtype)


def _hgrn(proj3, lb, norm_g, heads, c=HGRN_CHUNK, nb=4):
    b, s, n = proj3.shape
    dh = lb.shape[1]
    nblk = n // (2 * dh)
    nb = min(nb, b)
    assert b % nb == 0
    pm, masks = _hgrn_tables(c)
    return pl.pallas_call(
        functools.partial(_hgrn_kernel, heads=heads),
        grid=(b // nb, s // c),
        in_specs=[pl.BlockSpec((nb, c, 2 * dh), lambda i, j: (i, j, nblk - 2)),
                  pl.BlockSpec((nb, c, 2 * dh), lambda i, j: (i, j, nblk - 1)),
                  pl.BlockSpec((1, dh), lambda i, j: (0, 0)),
                  pl.BlockSpec((1, dh), lambda i, j: (0, 0)),
                  pl.BlockSpec(pm.shape, lambda i, j: (0, 0)),
                  pl.BlockSpec(masks.shape, lambda i, j: (0, 0, 0))],
        out_specs=pl.BlockSpec((nb, c, dh), lambda i, j: (i, j, 0)),
        out_shape=jax.ShapeDtypeStruct((b, s, dh), BF16),
        scratch_shapes=[pltpu.VMEM((nb, heads, dh // heads, dh // heads), F32)],
        compiler_params=_cparams(("arbitrary", "arbitrary")),
        name="hgrn2",
    )(proj3, proj3, lb, norm_g, pm, masks)


def _sort_network(n):
    pairs = []

    def merge(lo, cnt, r):
        step = 2 * r
        if step < cnt:
            merge(lo, cnt, step)
            merge(lo + r, cnt, step)
            pairs.extend((i, i + r) for i in range(lo + r, lo + cnt - r, step))
        else:
            pairs.append((lo, lo + r))

    def sort(lo, cnt):
        if cnt > 1:
            sort(lo, cnt // 2)
            sort(lo + cnt // 2, cnt // 2)
            merge(lo, cnt, 1)

    sort(0, n)
    return pairs


def _sort_blocks(vs, ids):
    vs, ids = list(vs), list(ids)
    for i, j in _sort_network(len(vs)):
        first = vs[i] >= vs[j]
        vs[i], vs[j] = jnp.maximum(vs[i], vs[j]), jnp.minimum(vs[i], vs[j])
        ids[i], ids[j] = jnp.where(first, ids[i], ids[j]), jnp.where(first, ids[j], ids[i])
    return vs, ids


def _merge_topk(vs, ids, k):
    vs, ids = list(vs), list(ids)
    out_v, out_i = [], []
    for r in range(k):
        mx = jnp.broadcast_to(jnp.max(vs[0], axis=0, keepdims=True), vs[0].shape)
        pick = jnp.where(vs[0] == mx, ids[0], jnp.int32(2 ** 30))
        pick = jnp.broadcast_to(jnp.min(pick, axis=0, keepdims=True), pick.shape)
        out_v.append(mx)
        out_i.append(pick)
        hit = ids[0] == pick
        for j in range(min(k - r - 1, len(vs) - 1)):
            vs[j] = jnp.where(hit, vs[j + 1], vs[j])
            ids[j] = jnp.where(hit, ids[j + 1], ids[j])
    return out_v, out_i


def _rows_from_blocks(blocks):
    sub = lax.broadcasted_iota(jnp.int32, blocks[0].shape, 0)
    out = []
    for g in range(0, len(blocks), SUBLANES):
        acc = blocks[g]
        for r in range(1, SUBLANES):
            acc = jnp.where(sub == r, blocks[g + r], acc)
        out.append(acc)
    return jnp.concatenate(out, axis=0)


def _mid_kernel(x_ref, ya_ref, yb_ref, woa_ref, wob_ref, g_ref, wq_ref, keys_ref,
                h2_ref, xn8_ref, eidx_ref, gate_ref, q_scr, e_scr, g_scr, *, n_keys):
    k = PEER_TOPK
    kq = int(np.sqrt(k))
    tm, d = h2_ref.shape
    heads = keys_ref.shape[0] // 2
    kd = keys_ref.shape[2]
    nblk = n_keys // SUBLANES

    h2 = (x_ref[...]
          + jnp.dot(ya_ref[...], woa_ref[...], preferred_element_type=F32)
          + jnp.dot(yb_ref[...], wob_ref[...], preferred_element_type=F32))
    h2_ref[...] = h2
    xn = h2 * lax.rsqrt(jnp.mean(h2 * h2, axis=-1, keepdims=True) + EPS) * g_ref[...]
    for c in range(d // LANES):
        xn8_ref[pl.ds(c, tm, stride=d // LANES), :] = xn[:, c * LANES:(c + 1) * LANES]
    q_scr[...] = jnp.dot(xn.astype(BF16), wq_ref[...], preferred_element_type=F32)

    sub = lax.broadcasted_iota(jnp.int32, (SUBLANES, tm), 0)
    low = sub < kq

    def head(h, carry):
        tops = []
        for p in range(2):
            col = pl.multiple_of((2 * h + p) * kd, kd)
            qhp = q_scr[:, pl.ds(col, kd)].astype(BF16)
            sc = lax.dot_general(keys_ref[2 * h + p], qhp, _NT,
                                 preferred_element_type=F32)
            vs = [sc[i * SUBLANES:(i + 1) * SUBLANES] for i in range(nblk)]
            ids = [sub + i * SUBLANES for i in range(nblk)]
            tops.append(_merge_topk(*_sort_blocks(vs, ids), k))
        (v1, i1), (v2, i2) = tops
        colv, coli = v1[0], i1[0]
        for s_ in range(1, 2 * kq):
            src_v, src_i = (v1[s_], i1[s_]) if s_ < kq else (v2[s_ - kq], i2[s_ - kq])
            colv = jnp.where(sub == s_, src_v, colv)
            coli = jnp.where(sub == s_, src_i, coli)
        cand, cid = [], []
        for j in range(k):
            if j + kq < k:
                cand.append(colv + jnp.where(low, v2[j], v1[j + kq]))
                cid.append(jnp.where(low, coli * n_keys + i2[j], i1[j + kq] * n_keys + coli))
            else:
                cand.append(jnp.where(low, colv + v2[j], -jnp.inf))
                cid.append(jnp.where(low, coli * n_keys + i2[j], -1))
        tv, te = _merge_topk(cand, cid, k)
        ez = [jnp.exp(v - tv[0]) for v in tv]
        den = ez[0]
        for e_ in ez[1:]:
            den = den + e_
        row0 = pl.multiple_of(h * k, k)
        e_scr[pl.ds(row0, k), :] = _rows_from_blocks(te) * _SLAB
        g_scr[pl.ds(row0, k), :] = _rows_from_blocks([e_ / den for e_ in ez])
        return carry

    lax.fori_loop(0, heads, head, 0)
    eidx_ref[...] = e_scr[...].T
    gate_ref[...] = g_scr[...].T


def _mid(x2, ya, yb, wo_bf, g, wq_bf, keys_bf, tm=256):
    t, d = x2.shape
    dc = ya.shape[1]
    hp, n_keys, kd = keys_bf.shape
    heads = hp // 2
    nq = wq_bf.shape[1]
    picks = heads * PEER_TOPK
    assert d == SUBLANES * LANES and 2 * int(np.sqrt(PEER_TOPK)) == SUBLANES
    return pl.pallas_call(
        functools.partial(_mid_kernel, n_keys=n_keys),
        grid=(t // tm,),
        in_specs=[pl.BlockSpec((tm, d), lambda i: (i, 0)),
                  pl.BlockSpec((tm, dc), lambda i: (i, 0)),
                  pl.BlockSpec((tm, d - dc), lambda i: (i, 0)),
                  pl.BlockSpec((dc, d), lambda i: (0, 0)),
                  pl.BlockSpec((d - dc, d), lambda i: (0, 0)),
                  pl.BlockSpec((1, d), lambda i: (0, 0)),
                  pl.BlockSpec((d, nq), lambda i: (0, 0)),
                  pl.BlockSpec((hp, n_keys, kd), lambda i: (0, 0, 0))],
        out_specs=[pl.BlockSpec((tm, d), lambda i: (i, 0)),
                   pl.BlockSpec((tm * SUBLANES, LANES), lambda i: (i, 0)),
                   pl.BlockSpec((tm, picks), lambda i: (i, 0)),
                   pl.BlockSpec((tm, picks), lambda i: (i, 0))],
        out_shape=[jax.ShapeDtypeStruct((t, d), F32),
                   jax.ShapeDtypeStruct((t * SUBLANES, LANES), F32),
                   jax.ShapeDtypeStruct((t, picks), jnp.int32),
                   jax.ShapeDtypeStruct((t, picks), F32)],
        scratch_shapes=[pltpu.VMEM((tm, nq), F32),
                        pltpu.VMEM((picks, tm), jnp.int32),
                        pltpu.VMEM((picks, tm), F32)],
        compiler_params=_cparams(("arbitrary",)),
        name="mid",
    )(x2, ya, yb, wo_bf[:dc], wo_bf[dc:], g, wq_bf, keys_bf)


def _peer_consts(picks):
    col = np.arange(picks * SUBLANES)
    diag = (col[None, :] % SUBLANES == np.arange(SUBLANES)[:, None]).astype(np.float32)
    grp = (col[:, None] // SUBLANES == np.arange(picks)[None, :]).astype(np.float32)
    return jnp.asarray(diag), jnp.asarray(grp, BF16), jnp.asarray(grp.T, BF16)


_PEER_UNROLL = 8


def _gather_rows(eidx_ref, tab_ref, gb_ref, t, picks):
    for j in range(picks):
        row = pl.multiple_of(eidx_ref[t, j], _SLAB)
        gb_ref[j * _SLAB:(j + 1) * _SLAB, :] = tab_ref[pl.ds(row, _SLAB), :]


def _token_pipeline(tb, gather, compute):
    u_n = _PEER_UNROLL
    for u in range(u_n):
        gather(u, u)

    def two_steps(i, carry):
        for half in range(2):
            first = (2 * i + half) * u_n
            for u in range(u_n):
                gather(jnp.minimum(first + u_n + u, tb - 1), (1 - half) * u_n + u)
            for u in range(u_n):
                compute(first + u, half * u_n + u)
        return carry

    lax.fori_loop(0, tb // (2 * u_n), two_steps, 0)


def _peer_down_kernel(eidx_ref, x_ref, gate_ref, tab_ref, diag_ref, grp_ref, w_ref, gb_ref, z_ref):
    tb, picks = gate_ref.shape

    def gather(t, slot):
        _gather_rows(eidx_ref, tab_ref, gb_ref.at[slot], t, picks)

    def compute(t, slot):
        w = pltpu.bitcast(gb_ref[slot], BF16)
        xhi, xlo = _split2(x_ref[t])
        part = (lax.dot_general(xhi, w, _NT, preferred_element_type=F32)
                + lax.dot_general(xlo, w, _NT, preferred_element_type=F32))
        z_ref[pl.ds(t, 1), :] = jnp.sum(part * diag_ref[...], axis=0, keepdims=True)

    _token_pipeline(tb, gather, compute)
    act = _dot2(z_ref[...], grp_ref[...])
    gelu = 0.5 * act * (1.0 + lax.erf(act * np.float32(np.sqrt(0.5))))
    w_ref[...] = gate_ref[...] * gelu


def _peer_up_kernel(eidx_ref, w_in_ref, h2_ref, g_ref, tab_ref, diag_ref, grpt_ref, o_ref,
                    gb_ref, wx_ref, y_ref):
    tb, picks = w_in_ref.shape
    d = o_ref.shape[1]
    wx_ref[...] = _dot2(w_in_ref[...], grpt_ref[...])

    def gather(t, slot):
        _gather_rows(eidx_ref, tab_ref, gb_ref.at[slot], t, picks)

    def compute(t, slot):
        v = pltpu.bitcast(gb_ref[slot], BF16)
        coef = wx_ref[pl.ds(t, 1), :] * diag_ref[...]
        chi, clo = _split2(coef)
        y_ref[pl.ds(pl.multiple_of(t * SUBLANES, SUBLANES), SUBLANES), :] = (
            jnp.dot(chi, v, preferred_element_type=F32)
            + jnp.dot(clo, v, preferred_element_type=F32))

    _token_pipeline(tb, gather, compute)
    ms = jnp.zeros((tb, 1), F32)
    for c in range(SUBLANES):
        cols = slice(c * LANES, (c + 1) * LANES)
        hc = h2_ref[:, cols] + y_ref[pl.ds(c, tb, stride=SUBLANES), :]
        ms = ms + jnp.sum(hc * hc, axis=-1, keepdims=True)
        o_ref[:, cols] = hc
    o_ref[...] = o_ref[...] * lax.rsqrt(ms / d + EPS) * g_ref[...]


def _table_spec(tab):
    return pl.BlockSpec(tab.shape, lambda i: (0, 0), pipeline_mode=pl.Buffered(1))


def _pack_kernel(t_ref, o_ref, z_ref):
    tr = t_ref.shape[0]
    for c in range(SUBLANES):
        z_ref[pl.ds(c, tr, stride=SUBLANES), :] = t_ref[:, c * LANES:(c + 1) * LANES]
    o_ref[...] = pltpu.bitcast(z_ref[...].astype(BF16), jnp.int32)


def _pack_table(tab, tr=256):
    n, d = tab.shape
    assert d == SUBLANES * LANES and n % tr == 0
    return pl.pallas_call(
        _pack_kernel,
        grid=(n // tr,),
        in_specs=[pl.BlockSpec((tr, d), lambda i: (i, 0))],
        out_specs=pl.BlockSpec((tr * _SLAB, LANES), lambda i: (i, 0)),
        out_shape=jax.ShapeDtypeStruct((n * _SLAB, LANES), jnp.int32),
        scratch_shapes=[pltpu.VMEM((tr * SUBLANES, LANES), F32)],
        compiler_params=_cparams(("arbitrary",)),
        name="pack_table",
    )(tab)


def _peer(xn8, h2, eidx_t, gate_t, u_tab, v_tab, final_g, tb=128):
    t, d = h2.shape
    picks = eidx_t.shape[1]
    assert d == SUBLANES * LANES and tb % (2 * _PEER_UNROLL) == 0
    diag, grp, grpt = _peer_consts(picks)
    x8 = xn8.reshape(t, SUBLANES, LANES)
    smem_idx = pl.BlockSpec((tb, picks), lambda i: (i, 0), memory_space=pltpu.SMEM)
    row3 = pl.BlockSpec((tb, SUBLANES, LANES), lambda i: (i, 0, 0))
    row2 = pl.BlockSpec((tb, picks), lambda i: (i, 0))
    rowd = pl.BlockSpec((tb, d), lambda i: (i, 0))
    const2 = lambda a: pl.BlockSpec(a.shape, lambda i: (0, 0))

    w = pl.pallas_call(
        _peer_down_kernel,
        grid=(t // tb,),
        in_specs=[smem_idx, row3, row2, _table_spec(u_tab), const2(diag), const2(grp)],
        out_specs=row2,
        out_shape=jax.ShapeDtypeStruct((t, picks), F32),
        scratch_shapes=[pltpu.VMEM((2 * _PEER_UNROLL, picks * _SLAB, LANES), jnp.int32),
                        pltpu.VMEM((tb, picks * SUBLANES), F32)],
        compiler_params=_cparams(("arbitrary",)),
        name="peer_down",
    )(eidx_t, x8, gate_t, u_tab, diag, grp)

    return pl.pallas_call(
        _peer_up_kernel,
        grid=(t // tb,),
        in_specs=[smem_idx, row2, rowd, const2(final_g), _table_spec(v_tab), const2(diag),
                  const2(grpt)],
        out_specs=rowd,
        out_shape=jax.ShapeDtypeStruct((t, d), F32),
        scratch_shapes=[pltpu.VMEM((2 * _PEER_UNROLL, picks * _SLAB, LANES), jnp.int32),
                        pltpu.VMEM((tb, picks * SUBLANES), F32),
                        pltpu.VMEM((tb * SUBLANES, LANES), F32)],
        compiler_params=_cparams(("arbitrary",)),
        name="peer_up",
    )(eidx_t, w, h2, final_g, v_tab, diag, grpt)


def kernel(x, norm_mix_g, w_in, conv_w, conv_b, conv_norm_g, conv_norm_b, hgrn_lb_logits,
           hgrn_norm_g, w_out, norm_ffn_g, peer_w_query, peer_sub_keys, peer_u, peer_v,
           final_norm_g):
    bsz, s, d = x.shape
    depth = w_in.shape[0]
    dc = conv_w.shape[2]
    dh = hgrn_norm_g.shape[1]
    heads_h = dh // LANES
    ph, _, n_keys, kd = peer_sub_keys.shape[1:]
    t = bsz * s
    row = lambda a: a.reshape(1, -1).astype(F32)

    lb_all = jnp.cumsum(jax.nn.softmax(hgrn_lb_logits.astype(F32), axis=0), axis=0)
    h = x.reshape(t, d)
    for l in range(depth):
        proj = _inproj(h, row(norm_mix_g[l]), w_in[l].astype(BF16))
        proj3 = proj.reshape(bsz, s, -1)
        ya = _conv(proj3, conv_w[l], row(conv_b[l]), row(conv_norm_g[l]), row(conv_norm_b[l]))
        yb = _hgrn(proj3, row(lb_all[l]), row(hgrn_norm_g[l]), heads_h)
        h2, xn8, offs, gate = _mid(h, ya.reshape(t, dc), yb.reshape(t, dh), w_out[l].astype(BF16),
                                   row(norm_ffn_g[l]), peer_w_query[l].astype(BF16),
                                   peer_sub_keys[l].reshape(ph * 2, n_keys, kd).astype(BF16))
        assert l == depth - 1, "a deeper stack needs the un-normalised residual between layers"
        h = _peer(xn8, h2, offs, gate, _pack_table(peer_u[l]), _pack_table(peer_v[l]),
                  row(final_norm_g))
    return h.reshape(bsz, s, d)
```

```python
import functools

import numpy as np
import jax
import jax.numpy as jnp
from jax import lax
from jax.experimental import pallas as pl
from jax.experimental.pallas import tpu as pltpu

F32 = jnp.float32
BF16 = jnp.bfloat16
EPS = 1e-6

LANES = 128
SUBLANES = 8
_SLAB = SUBLANES // 2
CONV_GROUPS = 8
HGRN_CHUNK = 128
PEER_TOPK = 16
VMEM_LIMIT = 48 * 1024 * 1024

_NT = (((1,), (1,)), ((), ()))
_TN = (((0,), (0,)), ((), ()))


def _cparams(sem):
    return pltpu.CompilerParams(dimension_semantics=sem, vmem_limit_bytes=VMEM_LIMIT)


def _split2(x):
    hi = x.astype(BF16)
    lo = (x - hi.astype(F32)).astype(BF16)
    return hi, lo


def _dot2(x, w):
    hi, lo = _split2(x)
    return (jnp.dot(hi, w, preferred_element_type=F32)
            + jnp.dot(lo, w, preferred_element_type=F32))


def _inproj_kernel(x_ref, g_ref, w_ref, o_ref):
    x = x_ref[...]
    xn = x * lax.rsqrt(jnp.mean(x * x, axis=-1, keepdims=True) + EPS) * g_ref[...]
    o_ref[...] = jnp.dot(xn.astype(BF16), w_ref[...], preferred_element_type=F32)


def _inproj(x2, g, w_bf, tm=256):
    t, d = x2.shape
    n = w_bf.shape[1]
    return pl.pallas_call(
        _inproj_kernel,
        grid=(t // tm,),
        in_specs=[pl.BlockSpec((tm, d), lambda i: (i, 0)),
                  pl.BlockSpec((1, d), lambda i: (0, 0)),
                  pl.BlockSpec((d, n), lambda i: (0, 0))],
        out_specs=pl.BlockSpec((tm, n), lambda i: (i, 0)),
        out_shape=jax.ShapeDtypeStruct((t, n), F32),
        compiler_params=_cparams(("arbitrary",)),
        name="inproj",
    )(x2, g, w_bf)


_CONV_SUB = 32
_CONV_HIST = 32


def _conv_kernel(p_ref, w_ref, cb_ref, ng_ref, nb_ref, gm_ref, o_ref, hbuf, pre, *, width):
    tm, dc = pre.shape
    j = pl.program_id(1)

    @pl.when(j == 0)
    def _():
        hbuf[0:_CONV_HIST, :] = jnp.zeros((_CONV_HIST, dc), F32)

    @pl.when(j > 0)
    def _():
        hbuf[0:_CONV_HIST, :] = hbuf[tm:tm + _CONV_HIST, :]

    a = p_ref[:, 0:dc]
    gate = p_ref[:, dc:2 * dc]
    hbuf[_CONV_HIST:_CONV_HIST + tm, :] = a * jax.nn.sigmoid(gate)

    first = _CONV_HIST - (width - 1)

    def body(r, carry):
        base = pl.multiple_of(r * _CONV_SUB, _CONV_SUB)
        win = hbuf[pl.ds(base, 2 * _CONV_SUB), :]
        acc = jnp.broadcast_to(cb_ref[...], (_CONV_SUB, dc))
        for ph in range(SUBLANES):
            offs = [first + k for k in range(width) if (first + k) % SUBLANES == ph]
            if not offs:
                continue
            span = max(offs) - ph + _CONV_SUB
            sh = win[ph:ph + span, :]
            for o in offs:
                k = o - first
                acc = acc + sh[o - ph:o - ph + _CONV_SUB, :] * w_ref[k:k + 1, :]
        pre[pl.ds(base, _CONV_SUB), :] = acc
        return carry

    lax.fori_loop(0, tm // _CONV_SUB, body, 0)

    h = pre[...]
    gm = gm_ref[...]
    mu = _dot2(h, gm)
    d = h - mu
    var = _dot2(d * d, gm)
    hn = d * lax.rsqrt(var + EPS) * ng_ref[...] + nb_ref[...]
    o_ref[...] = (hn * jax.nn.sigmoid(hn)).astype(o_ref.dtype)


def _conv(proj3, conv_w, conv_b, norm_g, norm_b, tm=512):
    b, s, _ = proj3.shape
    width, dc = conv_w.shape
    assert width - 1 <= _CONV_HIST and tm % _CONV_SUB == 0 and s % tm == 0
    gsz = dc // CONV_GROUPS
    gid = np.arange(dc) // gsz
    gm = jnp.asarray((gid[:, None] == gid[None, :]).astype(np.float32) / gsz, BF16)
    return pl.pallas_call(
        functools.partial(_conv_kernel, width=width),
        grid=(b, s // tm),
        in_specs=[pl.BlockSpec((None, tm, 2 * dc), lambda i, j: (i, j, 0)),
                  pl.BlockSpec((width, dc), lambda i, j: (0, 0)),
                  pl.BlockSpec((1, dc), lambda i, j: (0, 0)),
                  pl.BlockSpec((1, dc), lambda i, j: (0, 0)),
                  pl.BlockSpec((1, dc), lambda i, j: (0, 0)),
                  pl.BlockSpec((dc, dc), lambda i, j: (0, 0))],
        out_specs=pl.BlockSpec((None, tm, dc), lambda i, j: (i, j, 0)),
        out_shape=jax.ShapeDtypeStruct((b, s, dc), BF16),
        scratch_shapes=[pltpu.VMEM((tm + _CONV_HIST, dc), F32),
                        pltpu.VMEM((tm, dc), F32)],
        compiler_params=_cparams(("arbitrary", "arbitrary")),
        name="conformer_conv",
    )(proj3, conv_w, conv_b, norm_g, norm_b, gm)


def _hgrn_tables(c):
    r = np.arange(c)
    mats = [(r[None, :] <= r[:, None]), (r[None, :] > r[:, None])]
    masks = []
    m = 1
    while m < c:
        blk, pos = r // (2 * m), r % (2 * m)
        anchor = blk * 2 * m + m - 1
        second = pos >= m
        p = np.where(second[:, None],
                     (r[None, :] > anchor[:, None]) & (r[None, :] <= r[:, None]),
                     (r[None, :] > r[:, None]) & (r[None, :] <= anchor[:, None]))
        mats.append(p)
        masks.append((blk[:, None] == blk[None, :]) & second[:, None] & (~second)[None, :])
        m *= 2
    pm = np.concatenate(mats, axis=0).astype(np.float32)
    return jnp.asarray(pm, BF16), jnp.asarray(np.stack(masks).astype(np.float32))


def _hgrn_kernel(qf_ref, ig_ref, lb_ref, ng_ref, pm_ref, mask_ref, o_ref, st_ref, *, heads):
    @pl.when(pl.program_id(1) == 0)
    def _():
        st_ref[...] = jnp.zeros(st_ref.shape, F32)

    for bi in range(o_ref.shape[0]):
        _hgrn_chunk(qf_ref.at[bi], ig_ref.at[bi], lb_ref, ng_ref, pm_ref, mask_ref,
                    o_ref.at[bi], st_ref.at[bi], heads)


def _hgrn_chunk(qf_ref, ig_ref, lb_ref, ng_ref, pm_ref, mask_ref, o_ref, st_ref, heads):
    c, dh = o_ref.shape
    hd = dh // heads
    nlev = mask_ref.shape[0]

    lb = lb_ref[...]
    qin = qf_ref[:, 0:dh]
    z = qf_ref[:, dh:2 * dh]
    vin = ig_ref[:, 0:dh]
    gin = ig_ref[:, dh:2 * dh]

    f = lb + (1.0 - lb) * jax.nn.sigmoid(z)
    logf = jnp.log(f)
    kk = (1.0 - lb) * jax.nn.sigmoid(-z)
    q = qin * jax.nn.sigmoid(qin)

    pm = pm_ref[...]
    hi = logf.astype(BF16)
    r1 = logf - hi.astype(F32)
    mid = r1.astype(BF16)
    lo = (r1 - mid.astype(F32)).astype(BF16)
    ex = (jnp.dot(pm, hi, preferred_element_type=F32)
          + jnp.dot(pm, mid, preferred_element_type=F32)
          + jnp.dot(pm, lo, preferred_element_type=F32))
    b = ex[0:c]
    suf = ex[c:2 * c]

    for h in range(heads):
        sl = slice(h * hd, (h + 1) * hd)
        qh, kh, vh = q[:, sl], kk[:, sl], vin[:, sl]
        vb = vh.astype(BF16)
        st = st_ref[h]
        qb = (qh * jnp.exp(b[:, sl])).astype(BF16)
        o = lax.dot_general(qb, st.astype(BF16), _NT, preferred_element_type=F32)
        scores = jnp.zeros((c, c), F32)
        for lv in range(nlev):
            el = jnp.exp(ex[(2 + lv) * c:(3 + lv) * c, sl])
            s = lax.dot_general((qh * el).astype(BF16), (kh * el).astype(BF16), _NT,
                                preferred_element_type=F32)
            scores = scores + mask_ref[lv] * s
        o = o + jnp.dot(scores.astype(BF16), vb, preferred_element_type=F32)
        o = o + jnp.sum(qh * kh, axis=-1, keepdims=True) * vh
        kd = (kh * jnp.exp(suf[:, sl])).astype(BF16)
        st_ref[h] = (st * jnp.exp(b[c - 1:c, sl])
                     + lax.dot_general(vb, kd, _TN, preferred_element_type=F32))
        o = o * lax.rsqrt(jnp.mean(o * o, axis=-1, keepdims=True) + EPS) * ng_ref[:, sl]
        gh = gin[:, sl]
        o_ref[:, sl] = (o * (gh * jax.nn.sigmoid(gh))).astype(o_ref.dtype)


def _hgrn(proj3, lb, norm_g, heads, c=HGRN_CHUNK, nb=4):
    b, s, n = proj3.shape
    dh = lb.shape[1]
    nblk = n // (2 * dh)
    nb = min(nb, b)
    assert b % nb == 0
    pm, masks = _hgrn_tables(c)
    return pl.pallas_call(
        functools.partial(_hgrn_kernel, heads=heads),
        grid=(b // nb, s // c),
        in_specs=[pl.BlockSpec((nb, c, 2 * dh), lambda i, j: (i, j, nblk - 2)),
                  pl.BlockSpec((nb, c, 2 * dh), lambda i, j: (i, j, nblk - 1)),
                  pl.BlockSpec((1, dh), lambda i, j: (0, 0)),
                  pl.BlockSpec((1, dh), lambda i, j: (0, 0)),
                  pl.BlockSpec(pm.shape, lambda i, j: (0, 0)),
                  pl.BlockSpec(masks.shape, lambda i, j: (0, 0, 0))],
        out_specs=pl.BlockSpec((nb, c, dh), lambda i, j: (i, j, 0)),
        out_shape=jax.ShapeDtypeStruct((b, s, dh), BF16),
        scratch_shapes=[pltpu.VMEM((nb, heads, dh // heads, dh // heads), F32)],
        compiler_params=_cparams(("arbitrary", "arbitrary")),
        name="hgrn2",
    )(proj3, proj3, lb, norm_g, pm, masks)


def _sort_network(n):
    pairs = []

    def merge(lo, cnt, r):
        step = 2 * r
        if step < cnt:
            merge(lo, cnt, step)
            merge(lo + r, cnt, step)
            pairs.extend((i, i + r) for i in range(lo + r, lo + cnt - r, step))
        else:
            pairs.append((lo, lo + r))

    def sort(lo, cnt):
        if cnt > 1:
            sort(lo, cnt // 2)
            sort(lo + cnt // 2, cnt // 2)
            merge(lo, cnt, 1)

    sort(0, n)
    return pairs


def _sort_blocks(vs, ids):
    vs, ids = list(vs), list(ids)
    for i, j in _sort_network(len(vs)):
        first = vs[i] >= vs[j]
        vs[i], vs[j] = jnp.maximum(vs[i], vs[j]), jnp.minimum(vs[i], vs[j])
        ids[i], ids[j] = jnp.where(first, ids[i], ids[j]), jnp.where(first, ids[j], ids[i])
    return vs, ids


def _merge_topk(vs, ids, k):
    vs, ids = list(vs), list(ids)
    out_v, out_i = [], []
    for r in range(k):
        mx = jnp.broadcast_to(jnp.max(vs[0], axis=0, keepdims=True), vs[0].shape)
        pick = jnp.where(vs[0] == mx, ids[0], jnp.int32(2 ** 30))
        pick = jnp.broadcast_to(jnp.min(pick, axis=0, keepdims=True), pick.shape)
        out_v.append(mx)
        out_i.append(pick)
        hit = ids[0] == pick
        for j in range(min(k - r - 1, len(vs) - 1)):
            vs[j] = jnp.where(hit, vs[j + 1], vs[j])
            ids[j] = jnp.where(hit, ids[j + 1], ids[j])
    return out_v, out_i


def _rows_from_blocks(blocks):
    sub = lax.broadcasted_iota(jnp.int32, blocks[0].shape, 0)
    out = []
    for g in range(0, len(blocks), SUBLANES):
        acc = blocks[g]
        for r in range(1, SUBLANES):
            acc = jnp.where(sub == r, blocks[g + r], acc)
        out.append(acc)
    return jnp.concatenate(out, axis=0)


def _mid_kernel(x_ref, ya_ref, yb_ref, woa_ref, wob_ref, g_ref, wq_ref, keys_ref,
                h2_ref, xn8_ref, eidx_ref, gate_ref, q_scr, e_scr, g_scr, *, n_keys):
    k = PEER_TOPK
    kq = int(np.sqrt(k))
    tm, d = h2_ref.shape
    heads = keys_ref.shape[0] // 2
    kd = keys_ref.shape[2]
    nblk = n_keys // SUBLANES

    h2 = (x_ref[...]
          + jnp.dot(ya_ref[...], woa_ref[...], preferred_element_type=F32)
          + jnp.dot(yb_ref[...], wob_ref[...], preferred_element_type=F32))
    h2_ref[...] = h2
    xn = h2 * lax.rsqrt(jnp.mean(h2 * h2, axis=-1, keepdims=True) + EPS) * g_ref[...]
    for c in range(d // LANES):
        xn8_ref[pl.ds(c, tm, stride=d // LANES), :] = xn[:, c * LANES:(c + 1) * LANES]
    q_scr[...] = jnp.dot(xn.astype(BF16), wq_ref[...], preferred_element_type=F32)

    sub = lax.broadcasted_iota(jnp.int32, (SUBLANES, tm), 0)
    low = sub < kq

    def head(h, carry):
        tops = []
        for p in range(2):
            col = pl.multiple_of((2 * h + p) * kd, kd)
            qhp = q_scr[:, pl.ds(col, kd)].astype(BF16)
            sc = lax.dot_general(keys_ref[2 * h + p], qhp, _NT,
                                 preferred_element_type=F32)
            vs = [sc[i * SUBLANES:(i + 1) * SUBLANES] for i in range(nblk)]
            ids = [sub + i * SUBLANES for i in range(nblk)]
            tops.append(_merge_topk(*_sort_blocks(vs, ids), k))
        (v1, i1), (v2, i2) = tops
        colv, coli = v1[0], i1[0]
        for s_ in range(1, 2 * kq):
            src_v, src_i = (v1[s_], i1[s_]) if s_ < kq else (v2[s_ - kq], i2[s_ - kq])
            colv = jnp.where(sub == s_, src_v, colv)
            coli = jnp.where(sub == s_, src_i, coli)
        cand, cid = [], []
        for j in range(k):
            if j + kq < k:
                cand.append(colv + jnp.where(low, v2[j], v1[j + kq]))
                cid.append(jnp.where(low, coli * n_keys + i2[j], i1[j + kq] * n_keys + coli))
            else:
                cand.append(jnp.where(low, colv + v2[j], -jnp.inf))
                cid.append(jnp.where(low, coli * n_keys + i2[j], -1))
        tv, te = _merge_topk(cand, cid, k)
        ez = [jnp.exp(v - tv[0]) for v in tv]
        den = ez[0]
        for e_ in ez[1:]:
            den = den + e_
        row0 = pl.multiple_of(h * k, k)
        e_scr[pl.ds(row0, k), :] = _rows_from_blocks(te) * _SLAB
        g_scr[pl.ds(row0, k), :] = _rows_from_blocks([e_ / den for e_ in ez])
        return carry

    lax.fori_loop(0, heads, head, 0)
    eidx_ref[...] = e_scr[...].T
    gate_ref[...] = g_scr[...].T


def _mid(x2, ya, yb, wo_bf, g, wq_bf, keys_bf, tm=256):
    t, d = x2.shape
    dc = ya.shape[1]
    hp, n_keys, kd = keys_bf.shape
    heads = hp // 2
    nq = wq_bf.shape[1]
    picks = heads * PEER_TOPK
    assert d == SUBLANES * LANES and 2 * int(np.sqrt(PEER_TOPK)) == SUBLANES
    return pl.pallas_call(
        functools.partial(_mid_kernel, n_keys=n_keys),
        grid=(t // tm,),
        in_specs=[pl.BlockSpec((tm, d), lambda i: (i, 0)),
                  pl.BlockSpec((tm, dc), lambda i: (i, 0)),
                  pl.BlockSpec((tm, d - dc), lambda i: (i, 0)),
                  pl.BlockSpec((dc, d), lambda i: (0, 0)),
                  pl.BlockSpec((d - dc, d), lambda i: (0, 0)),
                  pl.BlockSpec((1, d), lambda i: (0, 0)),
                  pl.BlockSpec((d, nq), lambda i: (0, 0)),
                  pl.BlockSpec((hp, n_keys, kd), lambda i: (0, 0, 0))],
        out_specs=[pl.BlockSpec((tm, d), lambda i: (i, 0)),
                   pl.BlockSpec((tm * SUBLANES, LANES), lambda i: (i, 0)),
                   pl.BlockSpec((tm, picks), lambda i: (i, 0)),
                   pl.BlockSpec((tm, picks), lambda i: (i, 0))],
        out_shape=[jax.ShapeDtypeStruct((t, d), F32),
                   jax.ShapeDtypeStruct((t * SUBLANES, LANES), F32),
                   jax.ShapeDtypeStruct((t, picks), jnp.int32),
                   jax.ShapeDtypeStruct((t, picks), F32)],
        scratch_shapes=[pltpu.VMEM((tm, nq), F32),
                        pltpu.VMEM((picks, tm), jnp.int32),
                        pltpu.VMEM((picks, tm), F32)],
        compiler_params=_cparams(("arbitrary",)),
        name="mid",
    )(x2, ya, yb, wo_bf[:dc], wo_bf[dc:], g, wq_bf, keys_bf)


def _peer_consts(picks):
    col = np.arange(picks * SUBLANES)
    diag = (col[None, :] % SUBLANES == np.arange(SUBLANES)[:, None]).astype(np.float32)
    grp = (col[:, None] // SUBLANES == np.arange(picks)[None, :]).astype(np.float32)
    return jnp.asarray(diag), jnp.asarray(grp, BF16), jnp.asarray(grp.T, BF16)


_PEER_UNROLL = 8


def _gather_rows(eidx_ref, tab_ref, gb_ref, t, picks):
    for j in range(picks):
        row = pl.multiple_of(eidx_ref[t, j], _SLAB)
        gb_ref[j * _SLAB:(j + 1) * _SLAB, :] = tab_ref[pl.ds(row, _SLAB), :]


def _token_pipeline(tb, gather, compute):
    u_n = _PEER_UNROLL
    for u in range(u_n):
        gather(u, u)

    def two_steps(i, carry):
        for half in range(2):
            first = (2 * i + half) * u_n
            for u in range(u_n):
                gather(jnp.minimum(first + u_n + u, tb - 1), (1 - half) * u_n + u)
            for u in range(u_n):
                compute(first + u, half * u_n + u)
        return carry

    lax.fori_loop(0, tb // (2 * u_n), two_steps, 0)


def _peer_down_kernel(eidx_ref, x_ref, gate_ref, tab_ref, diag_ref, grp_ref, w_ref, gb_ref, z_ref):
    tb, picks = gate_ref.shape

    def gather(t, slot):
        _gather_rows(eidx_ref, tab_ref, gb_ref.at[slot], t, picks)

    def compute(t, slot):
        w = pltpu.bitcast(gb_ref[slot], BF16)
        xhi, xlo = _split2(x_ref[t])
        part = (lax.dot_general(xhi, w, _NT, preferred_element_type=F32)
                + lax.dot_general(xlo, w, _NT, preferred_element_type=F32))
        z_ref[pl.ds(t, 1), :] = jnp.sum(part * diag_ref[...], axis=0, keepdims=True)

    _token_pipeline(tb, gather, compute)
    act = _dot2(z_ref[...], grp_ref[...])
    gelu = 0.5 * act * (1.0 + lax.erf(act * np.float32(np.sqrt(0.5))))
    w_ref[...] = gate_ref[...] * gelu


def _peer_up_kernel(eidx_ref, w_in_ref, h2_ref, g_ref, tab_ref, diag_ref, grpt_ref, o_ref,
                    gb_ref, wx_ref, y_ref):
    tb, picks = w_in_ref.shape
    d = o_ref.shape[1]
    wx_ref[...] = _dot2(w_in_ref[...], grpt_ref[...])

    def gather(t, slot):
        _gather_rows(eidx_ref, tab_ref, gb_ref.at[slot], t, picks)

    def compute(t, slot):
        v = pltpu.bitcast(gb_ref[slot], BF16)
        coef = wx_ref[pl.ds(t, 1), :] * diag_ref[...]
        chi, clo = _split2(coef)
        y_ref[pl.ds(pl.multiple_of(t * SUBLANES, SUBLANES), SUBLANES), :] = (
            jnp.dot(chi, v, preferred_element_type=F32)
            + jnp.dot(clo, v, preferred_element_type=F32))

    _token_pipeline(tb, gather, compute)
    ms = jnp.zeros((tb, 1), F32)
    for c in range(SUBLANES):
        cols = slice(c * LANES, (c + 1) * LANES)
        hc = h2_ref[:, cols] + y_ref[pl.ds(c, tb, stride=SUBLANES), :]
        ms = ms + jnp.sum(hc * hc, axis=-1, keepdims=True)
        o_ref[:, cols] = hc
    o_ref[...] = o_ref[...] * lax.rsqrt(ms / d + EPS) * g_ref[...]


def _table_spec(tab):
    return pl.BlockSpec(tab.shape, lambda i: (0, 0), pipeline_mode=pl.Buffered(1))


def _pack_kernel(t_ref, o_ref, z_ref):
    tr = t_ref.shape[0]
    for c in range(SUBLANES):
        z_ref[pl.ds(c, tr, stride=SUBLANES), :] = t_ref[:, c * LANES:(c + 1) * LANES]
    o_ref[...] = pltpu.bitcast(z_ref[...].astype(BF16), jnp.int32)


def _pack_table(tab, tr=256):
    n, d = tab.shape
    assert d == SUBLANES * LANES and n % tr == 0
    return pl.pallas_call(
        _pack_kernel,
        grid=(n // tr,),
        in_specs=[pl.BlockSpec((tr, d), lambda i: (i, 0))],
        out_specs=pl.BlockSpec((tr * _SLAB, LANES), lambda i: (i, 0)),
        out_shape=jax.ShapeDtypeStruct((n * _SLAB, LANES), jnp.int32),
        scratch_shapes=[pltpu.VMEM((tr * SUBLANES, LANES), F32)],
        compiler_params=_cparams(("arbitrary",)),
        name="pack_table",
    )(tab)


def _peer(xn8, h2, eidx_t, gate_t, u_tab, v_tab, final_g, tb=256):
    t, d = h2.shape
    picks = eidx_t.shape[1]
    assert d == SUBLANES * LANES and tb % (2 * _PEER_UNROLL) == 0
    diag, grp, grpt = _peer_consts(picks)
    x8 = xn8.reshape(t, SUBLANES, LANES)
    smem_idx = pl.BlockSpec((tb, picks), lambda i: (i, 0), memory_space=pltpu.SMEM)
    row3 = pl.BlockSpec((tb, SUBLANES, LANES), lambda i: (i, 0, 0))
    row2 = pl.BlockSpec((tb, picks), lambda i: (i, 0))
    rowd = pl.BlockSpec((tb, d), lambda i: (i, 0))
    const2 = lambda a: pl.BlockSpec(a.shape, lambda i: (0, 0))

    w = pl.pallas_call(
        _peer_down_kernel,
        grid=(t // tb,),
        in_specs=[smem_idx, row3, row2, _table_spec(u_tab), const2(diag), const2(grp)],
        out_specs=row2,
        out_shape=jax.ShapeDtypeStruct((t, picks), F32),
        scratch_shapes=[pltpu.VMEM((2 * _PEER_UNROLL, picks * _SLAB, LANES), jnp.int32),
                        pltpu.VMEM((tb, picks * SUBLANES), F32)],
        compiler_params=_cparams(("arbitrary",)),
        name="peer_down",
    )(eidx_t, x8, gate_t, u_tab, diag, grp)

    return pl.pallas_call(
        _peer_up_kernel,
        grid=(t // tb,),
        in_specs=[smem_idx, row2, rowd, const2(final_g), _table_spec(v_tab), const2(diag),
                  const2(grpt)],
        out_specs=rowd,
        out_shape=jax.ShapeDtypeStruct((t, d), F32),
        scratch_shapes=[pltpu.VMEM((2 * _PEER_UNROLL, picks * _SLAB, LANES), jnp.int32),
                        pltpu.VMEM((tb, picks * SUBLANES), F32),
                        pltpu.VMEM((tb * SUBLANES, LANES), F32)],
        compiler_params=_cparams(("arbitrary",)),
        name="peer_up",
    )(eidx_t, w, h2, final_g, v_tab, diag, grpt)


def kernel(x, norm_mix_g, w_in, conv_w, conv_b, conv_norm_g, conv_norm_b, hgrn_lb_logits,
           hgrn_norm_g, w_out, norm_ffn_g, peer_w_query, peer_sub_keys, peer_u, peer_v,
           final_norm_g):
    bsz, s, d = x.shape
    depth = w_in.shape[0]
    dc = conv_w.shape[2]
    dh = hgrn_norm_g.shape[1]
    heads_h = dh // LANES
    ph, _, n_keys, kd = peer_sub_keys.shape[1:]
    t = bsz * s
    row = lambda a: a.reshape(1, -1).astype(F32)

    lb_all = jnp.cumsum(jax.nn.softmax(hgrn_lb_logits.astype(F32), axis=0), axis=0)
    h = x.reshape(t, d)
    for l in range(depth):
        proj = _inproj(h, row(norm_mix_g[l]), w_in[l].astype(BF16))
        proj3 = proj.reshape(bsz, s, -1)
        ya = _conv(proj3, conv_w[l], row(conv_b[l]), row(conv_norm_g[l]), row(conv_norm_b[l]))
        yb = _hgrn(proj3, row(lb_all[l]), row(hgrn_norm_g[l]), heads_h)
        h2, xn8, offs, gate = _mid(h, ya.reshape(t, dc), yb.reshape(t, dh), w_out[l].astype(BF16),
                                   row(norm_ffn_g[l]), peer_w_query[l].astype(BF16),
                                   peer_sub_keys[l].reshape(ph * 2, n_keys, kd).astype(BF16))
        assert l == depth - 1, "a deeper stack needs the un-normalised residual between layers"
        h = _peer(xn8, h2, offs, gate, _pack_table(peer_u[l]), _pack_table(peer_v[l]),
                  row(final_norm_g))
    return h.reshape(bsz, s, d)
```

```python
import functools

import numpy as np
import jax
import jax.numpy as jnp
from jax import lax
from jax.experimental import pallas as pl
from jax.experimental.pallas import tpu as pltpu

F32 = jnp.float32
BF16 = jnp.bfloat16
EPS = 1e-6

LANES = 128
SUBLANES = 8
_SLAB = SUBLANES // 2
CONV_GROUPS = 8
HGRN_CHUNK = 128
PEER_TOPK = 16
VMEM_LIMIT = 56 * 1024 * 1024

_NT = (((1,), (1,)), ((), ()))
_TN = (((0,), (0,)), ((), ()))


def _cparams(sem):
    return pltpu.CompilerParams(dimension_semantics=sem, vmem_limit_bytes=VMEM_LIMIT)


def _split2(x):
    hi = x.astype(BF16)
    lo = (x - hi.astype(F32)).astype(BF16)
    return hi, lo


def _dot2(x, w):
    hi, lo = _split2(x)
    return (jnp.dot(hi, w, preferred_element_type=F32)
            + jnp.dot(lo, w, preferred_element_type=F32))


def _inproj_kernel(x_ref, g_ref, w_ref, o_ref):
    x = x_ref[...]
    xn = x * lax.rsqrt(jnp.mean(x * x, axis=-1, keepdims=True) + EPS) * g_ref[...]
    o_ref[...] = jnp.dot(xn.astype(BF16), w_ref[...], preferred_element_type=F32)


def _inproj(x2, g, w_bf, tm=256):
    t, d = x2.shape
    n = w_bf.shape[1]
    return pl.pallas_call(
        _inproj_kernel,
        grid=(t // tm,),
        in_specs=[pl.BlockSpec((tm, d), lambda i: (i, 0)),
                  pl.BlockSpec((1, d), lambda i: (0, 0)),
                  pl.BlockSpec((d, n), lambda i: (0, 0))],
        out_specs=pl.BlockSpec((tm, n), lambda i: (i, 0)),
        out_shape=jax.ShapeDtypeStruct((t, n), F32),
        compiler_params=_cparams(("arbitrary",)),
        name="inproj",
    )(x2, g, w_bf)


_CONV_SUB = 32
_CONV_HIST = 32


def _conv_kernel(p_ref, w_ref, cb_ref, ng_ref, nb_ref, gm_ref, o_ref, hbuf, pre, *, width):
    tm, dc = pre.shape
    j = pl.program_id(1)

    @pl.when(j == 0)
    def _():
        hbuf[0:_CONV_HIST, :] = jnp.zeros((_CONV_HIST, dc), F32)

    @pl.when(j > 0)
    def _():
        hbuf[0:_CONV_HIST, :] = hbuf[tm:tm + _CONV_HIST, :]

    a = p_ref[:, 0:dc]
    gate = p_ref[:, dc:2 * dc]
    hbuf[_CONV_HIST:_CONV_HIST + tm, :] = a * jax.nn.sigmoid(gate)

    first = _CONV_HIST - (width - 1)

    def body(r, carry):
        base = pl.multiple_of(r * _CONV_SUB, _CONV_SUB)
        win = hbuf[pl.ds(base, 2 * _CONV_SUB), :]
        acc = jnp.broadcast_to(cb_ref[...], (_CONV_SUB, dc))
        for ph in range(SUBLANES):
            offs = [first + k for k in range(width) if (first + k) % SUBLANES == ph]
            if not offs:
                continue
            span = max(offs) - ph + _CONV_SUB
            sh = win[ph:ph + span, :]
            for o in offs:
                k = o - first
                acc = acc + sh[o - ph:o - ph + _CONV_SUB, :] * w_ref[k:k + 1, :]
        pre[pl.ds(base, _CONV_SUB), :] = acc
        return carry

    lax.fori_loop(0, tm // _CONV_SUB, body, 0)

    h = pre[...]
    gm = gm_ref[...]
    mu = _dot2(h, gm)
    d = h - mu
    var = _dot2(d * d, gm)
    hn = d * lax.rsqrt(var + EPS) * ng_ref[...] + nb_ref[...]
    o_ref[...] = (hn * jax.nn.sigmoid(hn)).astype(o_ref.dtype)


def _conv(proj3, conv_w, conv_b, norm_g, norm_b, tm=512):
    b, s, _ = proj3.shape
    width, dc = conv_w.shape
    assert width - 1 <= _CONV_HIST and tm % _CONV_SUB == 0 and s % tm == 0
    gsz = dc // CONV_GROUPS
    gid = np.arange(dc) // gsz
    gm = jnp.asarray((gid[:, None] == gid[None, :]).astype(np.float32) / gsz, BF16)
    return pl.pallas_call(
        functools.partial(_conv_kernel, width=width),
        grid=(b, s // tm),
        in_specs=[pl.BlockSpec((None, tm, 2 * dc), lambda i, j: (i, j, 0)),
                  pl.BlockSpec((width, dc), lambda i, j: (0, 0)),
                  pl.BlockSpec((1, dc), lambda i, j: (0, 0)),
                  pl.BlockSpec((1, dc), lambda i, j: (0, 0)),
                  pl.BlockSpec((1, dc), lambda i, j: (0, 0)),
                  pl.BlockSpec((dc, dc), lambda i, j: (0, 0))],
        out_specs=pl.BlockSpec((None, tm, dc), lambda i, j: (i, j, 0)),
        out_shape=jax.ShapeDtypeStruct((b, s, dc), BF16),
        scratch_shapes=[pltpu.VMEM((tm + _CONV_HIST, dc), F32),
                        pltpu.VMEM((tm, dc), F32)],
        compiler_params=_cparams(("arbitrary", "arbitrary")),
        name="conformer_conv",
    )(proj3, conv_w, conv_b, norm_g, norm_b, gm)


def _hgrn_tables(c):
    r = np.arange(c)
    mats = [(r[None, :] <= r[:, None]), (r[None, :] > r[:, None])]
    masks = []
    m = 1
    while m < c:
        blk, pos = r // (2 * m), r % (2 * m)
        anchor = blk * 2 * m + m - 1
        second = pos >= m
        p = np.where(second[:, None],
                     (r[None, :] > anchor[:, None]) & (r[None, :] <= r[:, None]),
                     (r[None, :] > r[:, None]) & (r[None, :] <= anchor[:, None]))
        mats.append(p)
        masks.append((blk[:, None] == blk[None, :]) & second[:, None] & (~second)[None, :])
        m *= 2
    pm = np.concatenate(mats, axis=0).astype(np.float32)
    return jnp.asarray(pm, BF16), jnp.asarray(np.stack(masks).astype(np.float32))


def _hgrn_kernel(qf_ref, ig_ref, lb_ref, ng_ref, pm_ref, mask_ref, o_ref, st_ref, *, heads):
    @pl.when(pl.program_id(1) == 0)
    def _():
        st_ref[...] = jnp.zeros(st_ref.shape, F32)

    for bi in range(o_ref.shape[0]):
        _hgrn_chunk(qf_ref.at[bi], ig_ref.at[bi], lb_ref, ng_ref, pm_ref, mask_ref,
                    o_ref.at[bi], st_ref.at[bi], heads)


def _hgrn_chunk(qf_ref, ig_ref, lb_ref, ng_ref, pm_ref, mask_ref, o_ref, st_ref, heads):
    c, dh = o_ref.shape
    hd = dh // heads
    nlev = mask_ref.shape[0]

    lb = lb_ref[...]
    qin = qf_ref[:, 0:dh]
    z = qf_ref[:, dh:2 * dh]
    vin = ig_ref[:, 0:dh]
    gin = ig_ref[:, dh:2 * dh]

    f = lb + (1.0 - lb) * jax.nn.sigmoid(z)
    logf = jnp.log(f)
    kk = (1.0 - lb) * jax.nn.sigmoid(-z)
    q = qin * jax.nn.sigmoid(qin)

    pm = pm_ref[...]
    hi = logf.astype(BF16)
    r1 = logf - hi.astype(F32)
    mid = r1.astype(BF16)
    lo = (r1 - mid.astype(F32)).astype(BF16)
    ex = (jnp.dot(pm, hi, preferred_element_type=F32)
          + jnp.dot(pm, mid, preferred_element_type=F32)
          + jnp.dot(pm, lo, preferred_element_type=F32))
    b = ex[0:c]
    suf = ex[c:2 * c]

    for h in range(heads):
        sl = slice(h * hd, (h + 1) * hd)
        qh, kh, vh = q[:, sl], kk[:, sl], vin[:, sl]
        vb = vh.astype(BF16)
        st = st_ref[h]
        qb = (qh * jnp.exp(b[:, sl])).astype(BF16)
        o = lax.dot_general(qb, st.astype(BF16), _NT, preferred_element_type=F32)
        scores = jnp.zeros((c, c), F32)
        for lv in range(nlev):
            el = jnp.exp(ex[(2 + lv) * c:(3 + lv) * c, sl])
            s = lax.dot_general((qh * el).astype(BF16), (kh * el).astype(BF16), _NT,
                                preferred_element_type=F32)
            scores = scores + mask_ref[lv] * s
        o = o + jnp.dot(scores.astype(BF16), vb, preferred_element_type=F32)
        o = o + jnp.sum(qh * kh, axis=-1, keepdims=True) * vh
        kd = (kh * jnp.exp(suf[:, sl])).astype(BF16)
        st_ref[h] = (st * jnp.exp(b[c - 1:c, sl])
                     + lax.dot_general(vb, kd, _TN, preferred_element_type=F32))
        o = o * lax.rsqrt(jnp.mean(o * o, axis=-1, keepdims=True) + EPS) * ng_ref[:, sl]
        gh = gin[:, sl]
        o_ref[:, sl] = (o * (gh * jax.nn.sigmoid(gh))).astype(o_ref.dtype)


def _hgrn(proj3, lb, norm_g, heads, c=HGRN_CHUNK, nb=4):
    b, s, n = proj3.shape
    dh = lb.shape[1]
    nblk = n // (2 * dh)
    nb = min(nb, b)
    assert b % nb == 0
    pm, masks = _hgrn_tables(c)
    return pl.pallas_call(
        functools.partial(_hgrn_kernel, heads=heads),
        grid=(b // nb, s // c),
        in_specs=[pl.BlockSpec((nb, c, 2 * dh), lambda i, j: (i, j, nblk - 2)),
                  pl.BlockSpec((nb, c, 2 * dh), lambda i, j: (i, j, nblk - 1)),
                  pl.BlockSpec((1, dh), lambda i, j: (0, 0)),
                  pl.BlockSpec((1, dh), lambda i, j: (0, 0)),
                  pl.BlockSpec(pm.shape, lambda i, j: (0, 0)),
                  pl.BlockSpec(masks.shape, lambda i, j: (0, 0, 0))],
        out_specs=pl.BlockSpec((nb, c, dh), lambda i, j: (i, j, 0)),
        out_shape=jax.ShapeDtypeStruct((b, s, dh), BF16),
        scratch_shapes=[pltpu.VMEM((nb, heads, dh // heads, dh // heads), F32)],
        compiler_params=_cparams(("arbitrary", "arbitrary")),
        name="hgrn2",
    )(proj3, proj3, lb, norm_g, pm, masks)


def _sort_network(n):
    pairs = []

    def merge(lo, cnt, r):
        step = 2 * r
        if step < cnt:
            merge(lo, cnt, step)
            merge(lo + r, cnt, step)
            pairs.extend((i, i + r) for i in range(lo + r, lo + cnt - r, step))
        else:
            pairs.append((lo, lo + r))

    def sort(lo, cnt):
        if cnt > 1:
            sort(lo, cnt // 2)
            sort(lo + cnt // 2, cnt // 2)
            merge(lo, cnt, 1)

    sort(0, n)
    return pairs


def _sort_blocks(vs, ids):
    vs, ids = list(vs), list(ids)
    for i, j in _sort_network(len(vs)):
        first = vs[i] >= vs[j]
        vs[i], vs[j] = jnp.maximum(vs[i], vs[j]), jnp.minimum(vs[i], vs[j])
        ids[i], ids[j] = jnp.where(first, ids[i], ids[j]), jnp.where(first, ids[j], ids[i])
    return vs, ids


def _merge_topk(vs, ids, k):
    vs, ids = list(vs), list(ids)
    out_v, out_i = [], []
    for r in range(k):
        mx = jnp.broadcast_to(jnp.max(vs[0], axis=0, keepdims=True), vs[0].shape)
        pick = jnp.where(vs[0] == mx, ids[0], jnp.int32(2 ** 30))
        pick = jnp.broadcast_to(jnp.min(pick, axis=0, keepdims=True), pick.shape)
        out_v.append(mx)
        out_i.append(pick)
        hit = ids[0] == pick
        for j in range(min(k - r - 1, len(vs) - 1)):
            vs[j] = jnp.where(hit, vs[j + 1], vs[j])
            ids[j] = jnp.where(hit, ids[j + 1], ids[j])
    return out_v, out_i


def _rows_from_blocks(blocks):
    sub = lax.broadcasted_iota(jnp.int32, blocks[0].shape, 0)
    out = []
    for g in range(0, len(blocks), SUBLANES):
        acc = blocks[g]
        for r in range(1, SUBLANES):
            acc = jnp.where(sub == r, blocks[g + r], acc)
        out.append(acc)
    return jnp.concatenate(out, axis=0)


def _mid_kernel(x_ref, ya_ref, yb_ref, woa_ref, wob_ref, g_ref, wq_ref, keys_ref,
                h2_ref, xn8_ref, eidx_ref, gate_ref, q_scr, e_scr, g_scr, *, n_keys):
    k = PEER_TOPK
    kq = int(np.sqrt(k))
    tm, d = h2_ref.shape
    heads = keys_ref.shape[0] // 2
    kd = keys_ref.shape[2]
    nblk = n_keys // SUBLANES

    h2 = (x_ref[...]
          + jnp.dot(ya_ref[...], woa_ref[...], preferred_element_type=F32)
          + jnp.dot(yb_ref[...], wob_ref[...], preferred_element_type=F32))
    h2_ref[...] = h2
    xn = h2 * lax.rsqrt(jnp.mean(h2 * h2, axis=-1, keepdims=True) + EPS) * g_ref[...]
    for c in range(d // LANES):
        xn8_ref[pl.ds(c, tm, stride=d // LANES), :] = xn[:, c * LANES:(c + 1) * LANES]
    q_scr[...] = jnp.dot(xn.astype(BF16), wq_ref[...], preferred_element_type=F32)

    sub = lax.broadcasted_iota(jnp.int32, (SUBLANES, tm), 0)
    low = sub < kq

    def head(h, carry):
        tops = []
        for p in range(2):
            col = pl.multiple_of((2 * h + p) * kd, kd)
            qhp = q_scr[:, pl.ds(col, kd)].astype(BF16)
            sc = lax.dot_general(keys_ref[2 * h + p], qhp, _NT,
                                 preferred_element_type=F32)
            vs = [sc[i * SUBLANES:(i + 1) * SUBLANES] for i in range(nblk)]
            ids = [sub + i * SUBLANES for i in range(nblk)]
            tops.append(_merge_topk(*_sort_blocks(vs, ids), k))
        (v1, i1), (v2, i2) = tops
        colv, coli = v1[0], i1[0]
        for s_ in range(1, 2 * kq):
            src_v, src_i = (v1[s_], i1[s_]) if s_ < kq else (v2[s_ - kq], i2[s_ - kq])
            colv = jnp.where(sub == s_, src_v, colv)
            coli = jnp.where(sub == s_, src_i, coli)
        cand, cid = [], []
        for j in range(k):
            if j + kq < k:
                cand.append(colv + jnp.where(low, v2[j], v1[j + kq]))
                cid.append(jnp.where(low, coli * n_keys + i2[j], i1[j + kq] * n_keys + coli))
            else:
                cand.append(jnp.where(low, colv + v2[j], -jnp.inf))
                cid.append(jnp.where(low, coli * n_keys + i2[j], -1))
        tv, te = _merge_topk(cand, cid, k)
        ez = [jnp.exp(v - tv[0]) for v in tv]
        den = ez[0]
        for e_ in ez[1:]:
            den = den + e_
        row0 = pl.multiple_of(h * k, k)
        e_scr[pl.ds(row0, k), :] = _rows_from_blocks(te) * _SLAB
        g_scr[pl.ds(row0, k), :] = _rows_from_blocks([e_ / den for e_ in ez])
        return carry

    lax.fori_loop(0, heads, head, 0)
    eidx_ref[...] = e_scr[...].T
    gate_ref[...] = g_scr[...].T


def _mid(x2, ya, yb, wo_bf, g, wq_bf, keys_bf, tm=256):
    t, d = x2.shape
    dc = ya.shape[1]
    hp, n_keys, kd = keys_bf.shape
    heads = hp // 2
    nq = wq_bf.shape[1]
    picks = heads * PEER_TOPK
    assert d == SUBLANES * LANES and 2 * int(np.sqrt(PEER_TOPK)) == SUBLANES
    return pl.pallas_call(
        functools.partial(_mid_kernel, n_keys=n_keys),
        grid=(t // tm,),
        in_specs=[pl.BlockSpec((tm, d), lambda i: (i, 0)),
                  pl.BlockSpec((tm, dc), lambda i: (i, 0)),
                  pl.BlockSpec((tm, d - dc), lambda i: (i, 0)),
                  pl.BlockSpec((dc, d), lambda i: (0, 0)),
                  pl.BlockSpec((d - dc, d), lambda i: (0, 0)),
                  pl.BlockSpec((1, d), lambda i: (0, 0)),
                  pl.BlockSpec((d, nq), lambda i: (0, 0)),
                  pl.BlockSpec((hp, n_keys, kd), lambda i: (0, 0, 0))],
        out_specs=[pl.BlockSpec((tm, d), lambda i: (i, 0)),
                   pl.BlockSpec((tm * SUBLANES, LANES), lambda i: (i, 0)),
                   pl.BlockSpec((tm, picks), lambda i: (i, 0)),
                   pl.BlockSpec((tm, picks), lambda i: (i, 0))],
        out_shape=[jax.ShapeDtypeStruct((t, d), F32),
                   jax.ShapeDtypeStruct((t * SUBLANES, LANES), F32),
                   jax.ShapeDtypeStruct((t, picks), jnp.int32),
                   jax.ShapeDtypeStruct((t, picks), F32)],
        scratch_shapes=[pltpu.VMEM((tm, nq), F32),
                        pltpu.VMEM((picks, tm), jnp.int32),
                        pltpu.VMEM((picks, tm), F32)],
        compiler_params=_cparams(("arbitrary",)),
        name="mid",
    )(x2, ya, yb, wo_bf[:dc], wo_bf[dc:], g, wq_bf, keys_bf)


def _peer_consts(picks):
    col = np.arange(picks * SUBLANES)
    diag = (col[None, :] % SUBLANES == np.arange(SUBLANES)[:, None]).astype(np.float32)
    grp = (col[:, None] // SUBLANES == np.arange(picks)[None, :]).astype(np.float32)
    return jnp.asarray(diag), jnp.asarray(grp, BF16), jnp.asarray(grp.T, BF16)


_PEER_UNROLL = 8


def _gather_rows(eidx_ref, tab_ref, gb_ref, t, picks):
    for j in range(picks):
        row = pl.multiple_of(eidx_ref[t, j], _SLAB)
        gb_ref[j * _SLAB:(j + 1) * _SLAB, :] = tab_ref[pl.ds(row, _SLAB), :]


def _token_pipeline(tb, gather, compute):
    u_n = _PEER_UNROLL
    for u in range(u_n):
        gather(u, u)

    def two_steps(i, carry):
        for half in range(2):
            first = (2 * i + half) * u_n
            for u in range(u_n):
                gather(jnp.minimum(first + u_n + u, tb - 1), (1 - half) * u_n + u)
            for u in range(u_n):
                compute(first + u, half * u_n + u)
        return carry

    lax.fori_loop(0, tb // (2 * u_n), two_steps, 0)


def _peer_down_kernel(eidx_ref, x_ref, gate_ref, tab_ref, diag_ref, grp_ref, w_ref, gb_ref, z_ref):
    tb, picks = gate_ref.shape

    def gather(t, slot):
        _gather_rows(eidx_ref, tab_ref, gb_ref.at[slot], t, picks)

    def compute(t, slot):
        w = pltpu.bitcast(gb_ref[slot], BF16)
        xhi, xlo = _split2(x_ref[t])
        part = (lax.dot_general(xhi, w, _NT, preferred_element_type=F32)
                + lax.dot_general(xlo, w, _NT, preferred_element_type=F32))
        z_ref[pl.ds(t, 1), :] = jnp.sum(part * diag_ref[...], axis=0, keepdims=True)

    _token_pipeline(tb, gather, compute)
    act = _dot2(z_ref[...], grp_ref[...])
    gelu = 0.5 * act * (1.0 + lax.erf(act * np.float32(np.sqrt(0.5))))
    w_ref[...] = gate_ref[...] * gelu


def _peer_up_kernel(eidx_ref, w_in_ref, h2_ref, g_ref, tab_ref, diag_ref, grpt_ref, o_ref,
                    gb_ref, wx_ref, y_ref):
    tb, picks = w_in_ref.shape
    d = o_ref.shape[1]
    wx_ref[...] = _dot2(w_in_ref[...], grpt_ref[...])

    def gather(t, slot):
        _gather_rows(eidx_ref, tab_ref, gb_ref.at[slot], t, picks)

    def compute(t, slot):
        v = pltpu.bitcast(gb_ref[slot], BF16)
        coef = wx_ref[pl.ds(t, 1), :] * diag_ref[...]
        chi, clo = _split2(coef)
        y_ref[pl.ds(pl.multiple_of(t * SUBLANES, SUBLANES), SUBLANES), :] = (
            jnp.dot(chi, v, preferred_element_type=F32)
            + jnp.dot(clo, v, preferred_element_type=F32))

    _token_pipeline(tb, gather, compute)
    ms = jnp.zeros((tb, 1), F32)
    for c in range(SUBLANES):
        cols = slice(c * LANES, (c + 1) * LANES)
        hc = h2_ref[:, cols] + y_ref[pl.ds(c, tb, stride=SUBLANES), :]
        ms = ms + jnp.sum(hc * hc, axis=-1, keepdims=True)
        o_ref[:, cols] = hc
    o_ref[...] = o_ref[...] * lax.rsqrt(ms / d + EPS) * g_ref[...]


def _table_spec(tab):
    return pl.BlockSpec(tab.shape, lambda i: (0, 0), pipeline_mode=pl.Buffered(1))


def _pack_kernel(t_ref, o_ref, z_ref):
    tr = t_ref.shape[0]
    for c in range(SUBLANES):
        z_ref[pl.ds(c, tr, stride=SUBLANES), :] = t_ref[:, c * LANES:(c + 1) * LANES]
    o_ref[...] = pltpu.bitcast(z_ref[...].astype(BF16), jnp.int32)


def _pack_table(tab, tr=256):
    n, d = tab.shape
    assert d == SUBLANES * LANES and n % tr == 0
    return pl.pallas_call(
        _pack_kernel,
        grid=(n // tr,),
        in_specs=[pl.BlockSpec((tr, d), lambda i: (i, 0))],
        out_specs=pl.BlockSpec((tr * _SLAB, LANES), lambda i: (i, 0)),
        out_shape=jax.ShapeDtypeStruct((n * _SLAB, LANES), jnp.int32),
        scratch_shapes=[pltpu.VMEM((tr * SUBLANES, LANES), F32)],
        compiler_params=_cparams(("arbitrary",)),
        name="pack_table",
    )(tab)


def _peer(xn8, h2, eidx_t, gate_t, u_tab, v_tab, final_g, tb=512):
    t, d = h2.shape
    picks = eidx_t.shape[1]
    assert d == SUBLANES * LANES and tb % (2 * _PEER_UNROLL) == 0
    diag, grp, grpt = _peer_consts(picks)
    x8 = xn8.reshape(t, SUBLANES, LANES)
    smem_idx = pl.BlockSpec((tb, picks), lambda i: (i, 0), memory_space=pltpu.SMEM)
    row3 = pl.BlockSpec((tb, SUBLANES, LANES), lambda i: (i, 0, 0))
    row2 = pl.BlockSpec((tb, picks), lambda i: (i, 0))
    rowd = pl.BlockSpec((tb, d), lambda i: (i, 0))
    const2 = lambda a: pl.BlockSpec(a.shape, lambda i: (0, 0))

    w = pl.pallas_call(
        _peer_down_kernel,
        grid=(t // tb,),
        in_specs=[smem_idx, row3, row2, _table_spec(u_tab), const2(diag), const2(grp)],
        out_specs=row2,
        out_shape=jax.ShapeDtypeStruct((t, picks), F32),
        scratch_shapes=[pltpu.VMEM((2 * _PEER_UNROLL, picks * _SLAB, LANES), jnp.int32),
                        pltpu.VMEM((tb, picks * SUBLANES), F32)],
        compiler_params=_cparams(("arbitrary",)),
        name="peer_down",
    )(eidx_t, x8, gate_t, u_tab, diag, grp)

    return pl.pallas_call(
        _peer_up_kernel,
        grid=(t // tb,),
        in_specs=[smem_idx, row2, rowd, const2(final_g), _table_spec(v_tab), const2(diag),
                  const2(grpt)],
        out_specs=rowd,
        out_shape=jax.ShapeDtypeStruct((t, d), F32),
        scratch_shapes=[pltpu.VMEM((2 * _PEER_UNROLL, picks * _SLAB, LANES), jnp.int32),
                        pltpu.VMEM((tb, picks * SUBLANES), F32),
                        pltpu.VMEM((tb * SUBLANES, LANES), F32)],
        compiler_params=_cparams(("arbitrary",)),
        name="peer_up",
    )(eidx_t, w, h2, final_g, v_tab, diag, grpt)


def kernel(x, norm_mix_g, w_in, conv_w, conv_b, conv_norm_g, conv_norm_b, hgrn_lb_logits,
           hgrn_norm_g, w_out, norm_ffn_g, peer_w_query, peer_sub_keys, peer_u, peer_v,
           final_norm_g):
    bsz, s, d = x.shape
    depth = w_in.shape[0]
    dc = conv_w.shape[2]
    dh = hgrn_norm_g.shape[1]
    heads_h = dh // LANES
    ph, _, n_keys, kd = peer_sub_keys.shape[1:]
    t = bsz * s
    row = lambda a: a.reshape(1, -1).astype(F32)

    lb_all = jnp.cumsum(jax.nn.softmax(hgrn_lb_logits.astype(F32), axis=0), axis=0)
    h = x.reshape(t, d)
    for l in range(depth):
        proj = _inproj(h, row(norm_mix_g[l]), w_in[l].astype(BF16))
        proj3 = proj.reshape(bsz, s, -1)
        ya = _conv(proj3, conv_w[l], row(conv_b[l]), row(conv_norm_g[l]), row(conv_norm_b[l]))
        yb = _hgrn(proj3, row(lb_all[l]), row(hgrn_norm_g[l]), heads_h)
        h2, xn8, offs, gate = _mid(h, ya.reshape(t, dc), yb.reshape(t, dh), w_out[l].astype(BF16),
                                   row(norm_ffn_g[l]), peer_w_query[l].astype(BF16),
                                   peer_sub_keys[l].reshape(ph * 2, n_keys, kd).astype(BF16))
        assert l == depth - 1, "a deeper stack needs the un-normalised residual between layers"
        h = _peer(xn8, h2, offs, gate, _pack_table(peer_u[l]), _pack_table(peer_v[l]),
                  row(final_norm_g))
    return h.reshape(bsz, s, d)
```

```python
import functools

import numpy as np
import jax
import jax.numpy as jnp
from jax import lax
from jax.experimental import pallas as pl
from jax.experimental.pallas import tpu as pltpu

F32 = jnp.float32
BF16 = jnp.bfloat16
EPS = 1e-6

LANES = 128
SUBLANES = 8
_SLAB = SUBLANES // 2
CONV_GROUPS = 8
HGRN_CHUNK = 128
PEER_TOPK = 16
VMEM_LIMIT = 56 * 1024 * 1024

_NT = (((1,), (1,)), ((), ()))
_TN = (((0,), (0,)), ((), ()))


def _cparams(sem):
    return pltpu.CompilerParams(dimension_semantics=sem, vmem_limit_bytes=VMEM_LIMIT)


def _split2(x):
    hi = x.astype(BF16)
    lo = (x - hi.astype(F32)).astype(BF16)
    return hi, lo


def _dot2(x, w):
    hi, lo = _split2(x)
    return (jnp.dot(hi, w, preferred_element_type=F32)
            + jnp.dot(lo, w, preferred_element_type=F32))


def _inproj_kernel(x_ref, g_ref, w_ref, o_ref):
    x = x_ref[...]
    xn = x * lax.rsqrt(jnp.mean(x * x, axis=-1, keepdims=True) + EPS) * g_ref[...]
    o_ref[...] = jnp.dot(xn.astype(BF16), w_ref[...], preferred_element_type=F32)


def _inproj(x2, g, w_bf, tm=256):
    t, d = x2.shape
    n = w_bf.shape[1]
    return pl.pallas_call(
        _inproj_kernel,
        grid=(t // tm,),
        in_specs=[pl.BlockSpec((tm, d), lambda i: (i, 0)),
                  pl.BlockSpec((1, d), lambda i: (0, 0)),
                  pl.BlockSpec((d, n), lambda i: (0, 0))],
        out_specs=pl.BlockSpec((tm, n), lambda i: (i, 0)),
        out_shape=jax.ShapeDtypeStruct((t, n), F32),
        compiler_params=_cparams(("arbitrary",)),
        name="inproj",
    )(x2, g, w_bf)


_CONV_SUB = 32
_CONV_HIST = 32


def _conv_kernel(p_ref, w_ref, cb_ref, ng_ref, nb_ref, gm_ref, o_ref, hbuf, pre, *, width):
    tm, dc = pre.shape
    j = pl.program_id(1)

    @pl.when(j == 0)
    def _():
        hbuf[0:_CONV_HIST, :] = jnp.zeros((_CONV_HIST, dc), F32)

    @pl.when(j > 0)
    def _():
        hbuf[0:_CONV_HIST, :] = hbuf[tm:tm + _CONV_HIST, :]

    a = p_ref[:, 0:dc]
    gate = p_ref[:, dc:2 * dc]
    hbuf[_CONV_HIST:_CONV_HIST + tm, :] = a * jax.nn.sigmoid(gate)

    first = _CONV_HIST - (width - 1)

    def body(r, carry):
        base = pl.multiple_of(r * _CONV_SUB, _CONV_SUB)
        win = hbuf[pl.ds(base, 2 * _CONV_SUB), :]
        acc = jnp.broadcast_to(cb_ref[...], (_CONV_SUB, dc))
        for ph in range(SUBLANES):
            offs = [first + k for k in range(width) if (first + k) % SUBLANES == ph]
            if not offs:
                continue
            span = max(offs) - ph + _CONV_SUB
            sh = win[ph:ph + span, :]
            for o in offs:
                k = o - first
                acc = acc + sh[o - ph:o - ph + _CONV_SUB, :] * w_ref[k:k + 1, :]
        pre[pl.ds(base, _CONV_SUB), :] = acc
        return carry

    lax.fori_loop(0, tm // _CONV_SUB, body, 0)

    h = pre[...]
    gm = gm_ref[...]
    mu = _dot2(h, gm)
    d = h - mu
    var = _dot2(d * d, gm)
    hn = d * lax.rsqrt(var + EPS) * ng_ref[...] + nb_ref[...]
    o_ref[...] = (hn * jax.nn.sigmoid(hn)).astype(o_ref.dtype)


def _conv(proj3, conv_w, conv_b, norm_g, norm_b, tm=512):
    b, s, _ = proj3.shape
    width, dc = conv_w.shape
    assert width - 1 <= _CONV_HIST and tm % _CONV_SUB == 0 and s % tm == 0
    gsz = dc // CONV_GROUPS
    gid = np.arange(dc) // gsz
    gm = jnp.asarray((gid[:, None] == gid[None, :]).astype(np.float32) / gsz, BF16)
    return pl.pallas_call(
        functools.partial(_conv_kernel, width=width),
        grid=(b, s // tm),
        in_specs=[pl.BlockSpec((None, tm, 2 * dc), lambda i, j: (i, j, 0)),
                  pl.BlockSpec((width, dc), lambda i, j: (0, 0)),
                  pl.BlockSpec((1, dc), lambda i, j: (0, 0)),
                  pl.BlockSpec((1, dc), lambda i, j: (0, 0)),
                  pl.BlockSpec((1, dc), lambda i, j: (0, 0)),
                  pl.BlockSpec((dc, dc), lambda i, j: (0, 0))],
        out_specs=pl.BlockSpec((None, tm, dc), lambda i, j: (i, j, 0)),
        out_shape=jax.ShapeDtypeStruct((b, s, dc), BF16),
        scratch_shapes=[pltpu.VMEM((tm + _CONV_HIST, dc), F32),
                        pltpu.VMEM((tm, dc), F32)],
        compiler_params=_cparams(("arbitrary", "arbitrary")),
        name="conformer_conv",
    )(proj3, conv_w, conv_b, norm_g, norm_b, gm)


def _hgrn_tables(c):
    r = np.arange(c)
    mats = [(r[None, :] <= r[:, None]), (r[None, :] > r[:, None])]
    masks = []
    m = 1
    while m < c:
        blk, pos = r // (2 * m), r % (2 * m)
        anchor = blk * 2 * m + m - 1
        second = pos >= m
        p = np.where(second[:, None],
                     (r[None, :] > anchor[:, None]) & (r[None, :] <= r[:, None]),
                     (r[None, :] > r[:, None]) & (r[None, :] <= anchor[:, None]))
        mats.append(p)
        masks.append((blk[:, None] == blk[None, :]) & second[:, None] & (~second)[None, :])
        m *= 2
    pm = np.concatenate(mats, axis=0).astype(np.float32)
    return jnp.asarray(pm, BF16), jnp.asarray(np.stack(masks).astype(np.float32))


def _hgrn_kernel(qf_ref, ig_ref, lb_ref, ng_ref, pm_ref, mask_ref, o_ref, st_ref, *, heads):
    @pl.when(pl.program_id(1) == 0)
    def _():
        st_ref[...] = jnp.zeros(st_ref.shape, F32)

    for bi in range(o_ref.shape[0]):
        _hgrn_chunk(qf_ref.at[bi], ig_ref.at[bi], lb_ref, ng_ref, pm_ref, mask_ref,
                    o_ref.at[bi], st_ref.at[bi], heads)


def _hgrn_chunk(qf_ref, ig_ref, lb_ref, ng_ref, pm_ref, mask_ref, o_ref, st_ref, heads):
    c, dh = o_ref.shape
    hd = dh // heads
    nlev = mask_ref.shape[0]

    lb = lb_ref[...]
    qin = qf_ref[:, 0:dh]
    z = qf_ref[:, dh:2 * dh]
    vin = ig_ref[:, 0:dh]
    gin = ig_ref[:, dh:2 * dh]

    f = lb + (1.0 - lb) * jax.nn.sigmoid(z)
    logf = jnp.log(f)
    kk = (1.0 - lb) * jax.nn.sigmoid(-z)
    q = qin * jax.nn.sigmoid(qin)

    pm = pm_ref[...]
    hi = logf.astype(BF16)
    r1 = logf - hi.astype(F32)
    mid = r1.astype(BF16)
    lo = (r1 - mid.astype(F32)).astype(BF16)
    ex = (jnp.dot(pm, hi, preferred_element_type=F32)
          + jnp.dot(pm, mid, preferred_element_type=F32)
          + jnp.dot(pm, lo, preferred_element_type=F32))
    b = ex[0:c]
    suf = ex[c:2 * c]

    for h in range(heads):
        sl = slice(h * hd, (h + 1) * hd)
        qh, kh, vh = q[:, sl], kk[:, sl], vin[:, sl]
        vb = vh.astype(BF16)
        st = st_ref[h]
        qb = (qh * jnp.exp(b[:, sl])).astype(BF16)
        o = lax.dot_general(qb, st.astype(BF16), _NT, preferred_element_type=F32)
        scores = jnp.zeros((c, c), F32)
        for lv in range(nlev):
            el = jnp.exp(ex[(2 + lv) * c:(3 + lv) * c, sl])
            s = lax.dot_general((qh * el).astype(BF16), (kh * el).astype(BF16), _NT,
                                preferred_element_type=F32)
            scores = scores + mask_ref[lv] * s
        o = o + jnp.dot(scores.astype(BF16), vb, preferred_element_type=F32)
        o = o + jnp.sum(qh * kh, axis=-1, keepdims=True) * vh
        kd = (kh * jnp.exp(suf[:, sl])).astype(BF16)
        st_ref[h] = (st * jnp.exp(b[c - 1:c, sl])
                     + lax.dot_general(vb, kd, _TN, preferred_element_type=F32))
        o = o * lax.rsqrt(jnp.mean(o * o, axis=-1, keepdims=True) + EPS) * ng_ref[:, sl]
        gh = gin[:, sl]
        o_ref[:, sl] = (o * (gh * jax.nn.sigmoid(gh))).astype(o_ref.dtype)


def _hgrn(proj3, lb, norm_g, heads, c=HGRN_CHUNK, nb=4):
    b, s, n = proj3.shape
    dh = lb.shape[1]
    nblk = n // (2 * dh)
    nb = min(nb, b)
    assert b % nb == 0
    pm, masks = _hgrn_tables(c)
    return pl.pallas_call(
        functools.partial(_hgrn_kernel, heads=heads),
        grid=(b // nb, s // c),
        in_specs=[pl.BlockSpec((nb, c, 2 * dh), lambda i, j: (i, j, nblk - 2)),
                  pl.BlockSpec((nb, c, 2 * dh), lambda i, j: (i, j, nblk - 1)),
                  pl.BlockSpec((1, dh), lambda i, j: (0, 0)),
                  pl.BlockSpec((1, dh), lambda i, j: (0, 0)),
                  pl.BlockSpec(pm.shape, lambda i, j: (0, 0)),
                  pl.BlockSpec(masks.shape, lambda i, j: (0, 0, 0))],
        out_specs=pl.BlockSpec((nb, c, dh), lambda i, j: (i, j, 0)),
        out_shape=jax.ShapeDtypeStruct((b, s, dh), BF16),
        scratch_shapes=[pltpu.VMEM((nb, heads, dh // heads, dh // heads), F32)],
        compiler_params=_cparams(("arbitrary", "arbitrary")),
        name="hgrn2",
    )(proj3, proj3, lb, norm_g, pm, masks)


def _sort_network(n):
    pairs = []

    def merge(lo, cnt, r):
        step = 2 * r
        if step < cnt:
            merge(lo, cnt, step)
            merge(lo + r, cnt, step)
            pairs.extend((i, i + r) for i in range(lo + r, lo + cnt - r, step))
        else:
            pairs.append((lo, lo + r))

    def sort(lo, cnt):
        if cnt > 1:
            sort(lo, cnt // 2)
            sort(lo + cnt // 2, cnt // 2)
            merge(lo, cnt, 1)

    sort(0, n)
    return pairs


def _sort_blocks(vs, ids):
    vs, ids = list(vs), list(ids)
    for i, j in _sort_network(len(vs)):
        first = vs[i] >= vs[j]
        vs[i], vs[j] = jnp.maximum(vs[i], vs[j]), jnp.minimum(vs[i], vs[j])
        ids[i], ids[j] = jnp.where(first, ids[i], ids[j]), jnp.where(first, ids[j], ids[i])
    return vs, ids


def _merge_topk(vs, ids, k):
    vs, ids = list(vs), list(ids)
    out_v, out_i = [], []
    for r in range(k):
        mx = jnp.broadcast_to(jnp.max(vs[0], axis=0, keepdims=True), vs[0].shape)
        pick = jnp.where(vs[0] == mx, ids[0], jnp.int32(2 ** 30))
        pick = jnp.broadcast_to(jnp.min(pick, axis=0, keepdims=True), pick.shape)
        out_v.append(mx)
        out_i.append(pick)
        hit = ids[0] == pick
        for j in range(min(k - r - 1, len(vs) - 1)):
            vs[j] = jnp.where(hit, vs[j + 1], vs[j])
            ids[j] = jnp.where(hit, ids[j + 1], ids[j])
    return out_v, out_i


def _rows_from_blocks(blocks):
    sub = lax.broadcasted_iota(jnp.int32, blocks[0].shape, 0)
    out = []
    for g in range(0, len(blocks), SUBLANES):
        acc = blocks[g]
        for r in range(1, SUBLANES):
            acc = jnp.where(sub == r, blocks[g + r], acc)
        out.append(acc)
    return jnp.concatenate(out, axis=0)


def _mid_kernel(x_ref, ya_ref, yb_ref, woa_ref, wob_ref, g_ref, wq_ref, keys_ref,
                h2_ref, xn8_ref, eidx_ref, gate_ref, q_scr, e_scr, g_scr, *, n_keys):
    k = PEER_TOPK
    kq = int(np.sqrt(k))
    tm, d = h2_ref.shape
    heads = keys_ref.shape[0] // 2
    kd = keys_ref.shape[2]
    nblk = n_keys // SUBLANES

    h2 = (x_ref[...]
          + jnp.dot(ya_ref[...], woa_ref[...], preferred_element_type=F32)
          + jnp.dot(yb_ref[...], wob_ref[...], preferred_element_type=F32))
    h2_ref[...] = h2
    xn = h2 * lax.rsqrt(jnp.mean(h2 * h2, axis=-1, keepdims=True) + EPS) * g_ref[...]
    for c in range(d // LANES):
        xn8_ref[pl.ds(c, tm, stride=d // LANES), :] = xn[:, c * LANES:(c + 1) * LANES]
    q_scr[...] = jnp.dot(xn.astype(BF16), wq_ref[...], preferred_element_type=F32)

    sub = lax.broadcasted_iota(jnp.int32, (SUBLANES, tm), 0)
    low = sub < kq

    def head(h, carry):
        tops = []
        for p in range(2):
            col = pl.multiple_of((2 * h + p) * kd, kd)
            qhp = q_scr[:, pl.ds(col, kd)].astype(BF16)
            sc = lax.dot_general(keys_ref[2 * h + p], qhp, _NT,
                                 preferred_element_type=F32)
            vs = [sc[i * SUBLANES:(i + 1) * SUBLANES] for i in range(nblk)]
            ids = [sub + i * SUBLANES for i in range(nblk)]
            tops.append(_merge_topk(*_sort_blocks(vs, ids), k))
        (v1, i1), (v2, i2) = tops
        colv, coli = v1[0], i1[0]
        for s_ in range(1, 2 * kq):
            src_v, src_i = (v1[s_], i1[s_]) if s_ < kq else (v2[s_ - kq], i2[s_ - kq])
            colv = jnp.where(sub == s_, src_v, colv)
            coli = jnp.where(sub == s_, src_i, coli)
        cand, cid = [], []
        for j in range(k):
            if j + kq < k:
                cand.append(colv + jnp.where(low, v2[j], v1[j + kq]))
                cid.append(jnp.where(low, coli * n_keys + i2[j], i1[j + kq] * n_keys + coli))
            else:
                cand.append(jnp.where(low, colv + v2[j], -jnp.inf))
                cid.append(jnp.where(low, coli * n_keys + i2[j], -1))
        tv, te = _merge_topk(cand, cid, k)
        ez = [jnp.exp(v - tv[0]) for v in tv]
        den = ez[0]
        for e_ in ez[1:]:
            den = den + e_
        row0 = pl.multiple_of(h * k, k)
        e_scr[pl.ds(row0, k), :] = _rows_from_blocks(te) * _SLAB
        g_scr[pl.ds(row0, k), :] = _rows_from_blocks([e_ / den for e_ in ez])
        return carry

    lax.fori_loop(0, heads, head, 0)
    eidx_ref[...] = e_scr[...].T
    gate_ref[...] = g_scr[...].T


def _mid(x2, ya, yb, wo_bf, g, wq_bf, keys_bf, tm=256):
    t, d = x2.shape
    dc = ya.shape[1]
    hp, n_keys, kd = keys_bf.shape
    heads = hp // 2
    nq = wq_bf.shape[1]
    picks = heads * PEER_TOPK
    assert d == SUBLANES * LANES and 2 * int(np.sqrt(PEER_TOPK)) == SUBLANES
    return pl.pallas_call(
        functools.partial(_mid_kernel, n_keys=n_keys),
        grid=(t // tm,),
        in_specs=[pl.BlockSpec((tm, d), lambda i: (i, 0)),
                  pl.BlockSpec((tm, dc), lambda i: (i, 0)),
                  pl.BlockSpec((tm, d - dc), lambda i: (i, 0)),
                  pl.BlockSpec((dc, d), lambda i: (0, 0)),
                  pl.BlockSpec((d - dc, d), lambda i: (0, 0)),
                  pl.BlockSpec((1, d), lambda i: (0, 0)),
                  pl.BlockSpec((d, nq), lambda i: (0, 0)),
                  pl.BlockSpec((hp, n_keys, kd), lambda i: (0, 0, 0))],
        out_specs=[pl.BlockSpec((tm, d), lambda i: (i, 0)),
                   pl.BlockSpec((tm * SUBLANES, LANES), lambda i: (i, 0)),
                   pl.BlockSpec((tm, picks), lambda i: (i, 0)),
                   pl.BlockSpec((tm, picks), lambda i: (i, 0))],
        out_shape=[jax.ShapeDtypeStruct((t, d), F32),
                   jax.ShapeDtypeStruct((t * SUBLANES, LANES), F32),
                   jax.ShapeDtypeStruct((t, picks), jnp.int32),
                   jax.ShapeDtypeStruct((t, picks), F32)],
        scratch_shapes=[pltpu.VMEM((tm, nq), F32),
                        pltpu.VMEM((picks, tm), jnp.int32),
                        pltpu.VMEM((picks, tm), F32)],
        compiler_params=_cparams(("arbitrary",)),
        name="mid",
    )(x2, ya, yb, wo_bf[:dc], wo_bf[dc:], g, wq_bf, keys_bf)


def _peer_consts(picks):
    col = np.arange(picks * SUBLANES)
    diag = (col[None, :] % SUBLANES == np.arange(SUBLANES)[:, None]).astype(np.float32)
    grp = (col[:, None] // SUBLANES == np.arange(picks)[None, :]).astype(np.float32)
    return jnp.asarray(diag), jnp.asarray(grp, BF16), jnp.asarray(grp.T, BF16)


_PEER_UNROLL = 8


def _gather_rows(eidx_ref, tab_ref, gb_ref, t, picks):
    for j in range(picks):
        row = pl.multiple_of(eidx_ref[t, j], _SLAB)
        gb_ref[j * _SLAB:(j + 1) * _SLAB, :] = tab_ref[pl.ds(row, _SLAB), :]


def _token_pipeline(tb, gather, compute):
    u_n = _PEER_UNROLL
    for u in range(u_n):
        gather(u, u)

    def two_steps(i, carry):
        for half in range(2):
            first = (2 * i + half) * u_n
            for u in range(u_n):
                gather(jnp.minimum(first + u_n + u, tb - 1), (1 - half) * u_n + u)
            for u in range(u_n):
                compute(first + u, half * u_n + u)
        return carry

    lax.fori_loop(0, tb // (2 * u_n), two_steps, 0)


def _peer_down_kernel(eidx_ref, x_ref, gate_ref, tab_ref, diag_ref, grp_ref, w_ref, gb_ref, z_ref):
    tb, picks = gate_ref.shape

    def gather(t, slot):
        _gather_rows(eidx_ref, tab_ref, gb_ref.at[slot], t, picks)

    def compute(t, slot):
        w = pltpu.bitcast(gb_ref[slot], BF16)
        part = lax.dot_general(x_ref[t].astype(BF16), w, _NT,
                               preferred_element_type=F32)
        z_ref[pl.ds(t, 1), :] = jnp.sum(part * diag_ref[...], axis=0, keepdims=True)

    _token_pipeline(tb, gather, compute)
    act = _dot2(z_ref[...], grp_ref[...])
    gelu = 0.5 * act * (1.0 + lax.erf(act * np.float32(np.sqrt(0.5))))
    w_ref[...] = gate_ref[...] * gelu


def _peer_up_kernel(eidx_ref, w_in_ref, h2_ref, g_ref, tab_ref, diag_ref, grpt_ref, o_ref,
                    gb_ref, wx_ref, y_ref):
    tb, picks = w_in_ref.shape
    d = o_ref.shape[1]
    wx_ref[...] = _dot2(w_in_ref[...], grpt_ref[...])

    def gather(t, slot):
        _gather_rows(eidx_ref, tab_ref, gb_ref.at[slot], t, picks)

    def compute(t, slot):
        v = pltpu.bitcast(gb_ref[slot], BF16)
        coef = wx_ref[pl.ds(t, 1), :] * diag_ref[...]
        chi, clo = _split2(coef)
        y_ref[pl.ds(pl.multiple_of(t * SUBLANES, SUBLANES), SUBLANES), :] = (
            jnp.dot(chi, v, preferred_element_type=F32)
            + jnp.dot(clo, v, preferred_element_type=F32))

    _token_pipeline(tb, gather, compute)
    ms = jnp.zeros((tb, 1), F32)
    for c in range(SUBLANES):
        cols = slice(c * LANES, (c + 1) * LANES)
        hc = h2_ref[:, cols] + y_ref[pl.ds(c, tb, stride=SUBLANES), :]
        ms = ms + jnp.sum(hc * hc, axis=-1, keepdims=True)
        o_ref[:, cols] = hc
    o_ref[...] = o_ref[...] * lax.rsqrt(ms / d + EPS) * g_ref[...]


def _table_spec(tab):
    return pl.BlockSpec(tab.shape, lambda i: (0, 0), pipeline_mode=pl.Buffered(1))


def _pack_kernel(t_ref, o_ref, z_ref):
    tr = t_ref.shape[0]
    for c in range(SUBLANES):
        z_ref[pl.ds(c, tr, stride=SUBLANES), :] = t_ref[:, c * LANES:(c + 1) * LANES]
    o_ref[...] = pltpu.bitcast(z_ref[...].astype(BF16), jnp.int32)


def _pack_table(tab, tr=256):
    n, d = tab.shape
    assert d == SUBLANES * LANES and n % tr == 0
    return pl.pallas_call(
        _pack_kernel,
        grid=(n // tr,),
        in_specs=[pl.BlockSpec((tr, d), lambda i: (i, 0))],
        out_specs=pl.BlockSpec((tr * _SLAB, LANES), lambda i: (i, 0)),
        out_shape=jax.ShapeDtypeStruct((n * _SLAB, LANES), jnp.int32),
        scratch_shapes=[pltpu.VMEM((tr * SUBLANES, LANES), F32)],
        compiler_params=_cparams(("arbitrary",)),
        name="pack_table",
    )(tab)


def _peer(xn8, h2, eidx_t, gate_t, u_tab, v_tab, final_g, tb=512):
    t, d = h2.shape
    picks = eidx_t.shape[1]
    assert d == SUBLANES * LANES and tb % (2 * _PEER_UNROLL) == 0
    diag, grp, grpt = _peer_consts(picks)
    x8 = xn8.reshape(t, SUBLANES, LANES)
    smem_idx = pl.BlockSpec((tb, picks), lambda i: (i, 0), memory_space=pltpu.SMEM)
    row3 = pl.BlockSpec((tb, SUBLANES, LANES), lambda i: (i, 0, 0))
    row2 = pl.BlockSpec((tb, picks), lambda i: (i, 0))
    rowd = pl.BlockSpec((tb, d), lambda i: (i, 0))
    const2 = lambda a: pl.BlockSpec(a.shape, lambda i: (0, 0))

    w = pl.pallas_call(
        _peer_down_kernel,
        grid=(t // tb,),
        in_specs=[smem_idx, row3, row2, _table_spec(u_tab), const2(diag), const2(grp)],
        out_specs=row2,
        out_shape=jax.ShapeDtypeStruct((t, picks), F32),
        scratch_shapes=[pltpu.VMEM((2 * _PEER_UNROLL, picks * _SLAB, LANES), jnp.int32),
                        pltpu.VMEM((tb, picks * SUBLANES), F32)],
        compiler_params=_cparams(("arbitrary",)),
        name="peer_down",
    )(eidx_t, x8, gate_t, u_tab, diag, grp)

    return pl.pallas_call(
        _peer_up_kernel,
        grid=(t // tb,),
        in_specs=[smem_idx, row2, rowd, const2(final_g), _table_spec(v_tab), const2(diag),
                  const2(grpt)],
        out_specs=rowd,
        out_shape=jax.ShapeDtypeStruct((t, d), F32),
        scratch_shapes=[pltpu.VMEM((2 * _PEER_UNROLL, picks * _SLAB, LANES), jnp.int32),
                        pltpu.VMEM((tb, picks * SUBLANES), F32),
                        pltpu.VMEM((tb * SUBLANES, LANES), F32)],
        compiler_params=_cparams(("arbitrary",)),
        name="peer_up",
    )(eidx_t, w, h2, final_g, v_tab, diag, grpt)


def kernel(x, norm_mix_g, w_in, conv_w, conv_b, conv_norm_g, conv_norm_b, hgrn_lb_logits,
           hgrn_norm_g, w_out, norm_ffn_g, peer_w_query, peer_sub_keys, peer_u, peer_v,
           final_norm_g):
    bsz, s, d = x.shape
    depth = w_in.shape[0]
    dc = conv_w.shape[2]
    dh = hgrn_norm_g.shape[1]
    heads_h = dh // LANES
    ph, _, n_keys, kd = peer_sub_keys.shape[1:]
    t = bsz * s
    row = lambda a: a.reshape(1, -1).astype(F32)

    lb_all = jnp.cumsum(jax.nn.softmax(hgrn_lb_logits.astype(F32), axis=0), axis=0)
    h = x.reshape(t, d)
    for l in range(depth):
        proj = _inproj(h, row(norm_mix_g[l]), w_in[l].astype(BF16))
        proj3 = proj.reshape(bsz, s, -1)
        ya = _conv(proj3, conv_w[l], row(conv_b[l]), row(conv_norm_g[l]), row(conv_norm_b[l]))
        yb = _hgrn(proj3, row(lb_all[l]), row(hgrn_norm_g[l]), heads_h)
        h2, xn8, offs, gate = _mid(h, ya.reshape(t, dc), yb.reshape(t, dh), w_out[l].astype(BF16),
                                   row(norm_ffn_g[l]), peer_w_query[l].astype(BF16),
                                   peer_sub_keys[l].reshape(ph * 2, n_keys, kd).astype(BF16))
        assert l == depth - 1, "a deeper stack needs the un-normalised residual between layers"
        h = _peer(xn8, h2, offs, gate, _pack_table(peer_u[l]), _pack_table(peer_v[l]),
                  row(final_norm_g))
    return h.reshape(bsz, s, d)
```
